```python
import jax, jax.numpy as jnp
from jax import lax
import numpy as np

D_MODEL = 1024
BATCH = 32
SEQ = 2048
DEPTH = 1

GRID_W = 64
CTX_LEN = 256
MLA_HEADS = 8
QK_NOPE = 64
QK_ROPE = 32
QK_HEAD = QK_NOPE + QK_ROPE
V_HEAD = 64
Q_LORA = 256
KV_LORA = 128
AXIS_DIM = QK_ROPE // 2
ROPE_BASE = 10000.0
Q_BLOCK = 128
MLA_WIDTH = MLA_HEADS * V_HEAD
GMLP_GROUPS = 8
GMLP_GROUP_DIM = 64
GMLP_WIDTH = GMLP_GROUPS * GMLP_GROUP_DIM
CHUNK = 128
D_MIX = MLA_WIDTH + GMLP_WIDTH
KV_COLS = KV_LORA + QK_ROPE
Q_START = KV_COLS
U_START = KV_COLS + Q_LORA
V_START = U_START + GMLP_WIDTH
IN_COLS = V_START + GMLP_WIDTH
D_FF = 2816
N_MOD = 9
EPS = 1e-6

kernel_name = "hymba_mla_gmlp_macaron_dit_layer"


def rms_norm(x, w):
    xf = x.astype(jnp.float32)
    y = xf * lax.rsqrt(jnp.mean(xf * xf, axis=-1, keepdims=True) + EPS)
    return (y * w.astype(jnp.float32)).astype(x.dtype)


def modulate(h, shift, scale):
    return h * (1 + scale) + shift


def swiglu(h, w1, w3, w2):
    return (jax.nn.silu(h @ w1) * (h @ w3)) @ w2


def ffn_sublayer(h_in, shift, scale, gate, norm_w, w1, w3, w2):
    h = modulate(rms_norm(h_in, norm_w), shift, scale)
    return h_in + 0.5 * gate * swiglu(h, w1, w3, w2)


def axial_rope(x, cos, sin):
    xr = x.reshape(x.shape[:-1] + (2, 2, AXIS_DIM // 2))
    rot = jnp.stack([-xr[..., 1, :], xr[..., 0, :]], axis=-2).reshape(x.shape)
    return x * cos[:, None, :] + rot * sin[:, None, :]


def rope_part(x, rope):
    if rope is None:
        return x
    return jnp.concatenate([x[..., :QK_NOPE], axial_rope(x[..., QK_NOPE:], *rope)], axis=-1)


def mla_keys_values(kv_proj, kv_a_norm_w, w_ukv, k_norm_w, rope):
    B, S, _ = kv_proj.shape
    c_kv = rms_norm(kv_proj[..., :KV_LORA], kv_a_norm_w)
    k_pe = kv_proj[..., KV_LORA:]
    kv = (c_kv @ w_ukv).reshape(B, S, MLA_HEADS, QK_NOPE + V_HEAD)
    k_nope, v = kv[..., :QK_NOPE], kv[..., QK_NOPE:]
    k_pe = jnp.broadcast_to(k_pe[:, :, None, :], (B, S, MLA_HEADS, QK_ROPE))
    k = rms_norm(jnp.concatenate([k_nope, k_pe], axis=-1), k_norm_w)
    return rope_part(k, rope), v


def mla_queries(q_proj, q_a_norm_w, w_uq, q_norm_w, rope):
    B, S, _ = q_proj.shape
    c_q = rms_norm(q_proj, q_a_norm_w)
    q = (c_q @ w_uq).reshape(B, S, MLA_HEADS, QK_HEAD)
    return rope_part(rms_norm(q, q_norm_w), rope)


def block_attention(q, k_all, v_all):
    B, S, H, Dk = q.shape
    nb = S // Q_BLOCK
    scale = Dk ** -0.5
    qb = jnp.moveaxis(q.reshape(B, nb, Q_BLOCK, H, Dk), 1, 0)

    def one_block(q_blk):
        s = jnp.einsum('bqhd,bkhd->bhqk', q_blk, k_all).astype(jnp.float32) * scale
        p = jax.nn.softmax(s, axis=-1).astype(v_all.dtype)
        return jnp.einsum('bhqk,bkhd->bqhd', p, v_all)

    out = lax.map(one_block, qb)
    return jnp.moveaxis(out, 0, 1).reshape(B, S, H * V_HEAD)


def chunk_gmlp(u, v, v_norm_w, w_s, b_s):
    B, S, _ = u.shape
    n = S // CHUNK
    u = jax.nn.gelu(u).reshape(B, n, CHUNK, GMLP_GROUPS, GMLP_GROUP_DIM)
    v = rms_norm(jax.nn.gelu(v).reshape(B, n, CHUNK, GMLP_GROUPS, GMLP_GROUP_DIM), v_norm_w)
    s = jnp.einsum('gpq,bnqgc->bnpgc', w_s, v) + b_s.T[:, :, None]
    return (u * s).reshape(B, S, GMLP_WIDTH)


def token_mix(proj, k_all, v_all, rope, q_a_norm_w, w_uq, q_norm_w, v_norm_w, w_s, b_s, w_out):
    q = mla_queries(proj[..., Q_START:U_START], q_a_norm_w, w_uq, q_norm_w, rope)
    attn = block_attention(q, k_all, v_all)
    sg = chunk_gmlp(proj[..., U_START:V_START], proj[..., V_START:], v_norm_w, w_s, b_s)
    return jnp.concatenate([attn, sg], axis=-1) @ w_out


def hybrid_layer(x, ctx, c, c_ctx, cos, sin,
                 w_ada, b_ada, norm1_w, ffn1_w1, ffn1_w3, ffn1_w2,
                 norm2_w, w_in, q_a_norm_w, w_uq, kv_a_norm_w, w_ukv, q_norm_w, k_norm_w,
                 v_norm_w, w_s, b_s, w_out,
                 norm3_w, ffn2_w1, ffn2_w3, ffn2_w2, update_ctx):
    mx = jnp.split((jax.nn.silu(c) @ w_ada + b_ada)[:, None, :], N_MOD, axis=-1)
    mc = jnp.split((jax.nn.silu(c_ctx) @ w_ada + b_ada)[None, None, :], N_MOD, axis=-1)
    rope = (cos, sin)

    x = ffn_sublayer(x, mx[0], mx[1], mx[2], norm1_w, ffn1_w1, ffn1_w3, ffn1_w2)
    ctx = ffn_sublayer(ctx, mc[0], mc[1], mc[2], norm1_w, ffn1_w1, ffn1_w3, ffn1_w2)

    proj = modulate(rms_norm(x, norm2_w), mx[3], mx[4]) @ w_in
    hc = modulate(rms_norm(ctx, norm2_w), mc[3], mc[4])
    proj_c = hc @ (w_in if update_ctx else w_in[:, :KV_COLS])
    k_lat, v_lat = mla_keys_values(proj[..., :KV_COLS], kv_a_norm_w, w_ukv, k_norm_w, rope)
    k_ctx, v_ctx = mla_keys_values(proj_c[..., :KV_COLS], kv_a_norm_w, w_ukv, k_norm_w, None)
    k_all = jnp.concatenate([k_lat, k_ctx], axis=1)
    v_all = jnp.concatenate([v_lat, v_ctx], axis=1)
    x = x + mx[5] * token_mix(proj, k_all, v_all, rope, q_a_norm_w, w_uq, q_norm_w,
                              v_norm_w, w_s, b_s, w_out)
    if update_ctx:
        ctx = ctx + mc[5] * token_mix(proj_c, k_ctx, v_ctx, None, q_a_norm_w, w_uq, q_norm_w,
                                      v_norm_w, w_s, b_s, w_out)
        ctx = ffn_sublayer(ctx, mc[6], mc[7], mc[8], norm3_w, ffn2_w1, ffn2_w3, ffn2_w2)

    x = ffn_sublayer(x, mx[6], mx[7], mx[8], norm3_w, ffn2_w1, ffn2_w3, ffn2_w2)
    return x, ctx


def setup_inputs(seed: int = 0) -> dict:
    key = jax.random.key(seed)
    ks = jax.random.split(key, 26)
    f32 = jnp.float32

    def dense(k, shape, fan_in, gain=1.0):
        return jax.random.normal(k, shape, f32) * (gain * fan_in ** -0.5)

    def gain_vec(k, shape):
        return 1.0 + 0.02 * jax.random.normal(k, shape, f32)

    L = DEPTH
    return {
        "x": jax.random.normal(ks[0], (BATCH, SEQ, D_MODEL), f32),
        "c": jax.random.normal(ks[1], (BATCH, D_MODEL), f32),
        "ctx": jax.random.normal(ks[2], (BATCH, CTX_LEN, D_MODEL), f32),
        "c_ctx": jax.random.normal(ks[3], (D_MODEL,), f32),
        "w_ada": dense(ks[4], (L, D_MODEL, N_MOD * D_MODEL), D_MODEL, 0.5),
        "b_ada": 0.02 * jax.random.normal(ks[5], (L, N_MOD * D_MODEL), f32),
        "norm1_w": gain_vec(ks[6], (L, D_MODEL)),
        "ffn1_w1": dense(ks[7], (L, D_MODEL, D_FF), D_MODEL),
        "ffn1_w3": dense(ks[8], (L, D_MODEL, D_FF), D_MODEL),
        "ffn1_w2": dense(ks[9], (L, D_FF, D_MODEL), D_FF),
        "norm2_w": gain_vec(ks[10], (L, D_MODEL)),
        "w_in": dense(ks[11], (L, D_MODEL, IN_COLS), D_MODEL),
        "q_a_norm_w": gain_vec(ks[12], (L, Q_LORA)),
        "w_uq": dense(ks[13], (L, Q_LORA, MLA_HEADS * QK_HEAD), Q_LORA),
        "kv_a_norm_w": gain_vec(ks[14], (L, KV_LORA)),
        "w_ukv": dense(ks[15], (L, KV_LORA, MLA_HEADS * (QK_NOPE + V_HEAD)), KV_LORA),
        "q_norm_w": gain_vec(ks[16], (L, QK_HEAD)),
        "k_norm_w": gain_vec(ks[17], (L, QK_HEAD)),
        "v_norm_w": gain_vec(ks[18], (L, GMLP_GROUPS, GMLP_GROUP_DIM)),
        "w_s": dense(ks[19], (L, GMLP_GROUPS, CHUNK, CHUNK), CHUNK),
        "b_s": gain_vec(ks[20], (L, GMLP_GROUPS, CHUNK)),
        "w_out": dense(ks[21], (L, D_MIX, D_MODEL), D_MIX),
        "norm3_w": gain_vec(ks[22], (L, D_MODEL)),
        "ffn2_w1": dense(ks[23], (L, D_MODEL, D_FF), D_MODEL),
        "ffn2_w3": dense(ks[24], (L, D_MODEL, D_FF), D_MODEL),
        "ffn2_w2": dense(ks[25], (L, D_FF, D_MODEL), D_FF),
    }


def reference(x, c, ctx, c_ctx, w_ada, b_ada, norm1_w, ffn1_w1, ffn1_w3, ffn1_w2,
              norm2_w, w_in, q_a_norm_w, w_uq, kv_a_norm_w, w_ukv, q_norm_w, k_norm_w,
              v_norm_w, w_s, b_s, w_out, norm3_w, ffn2_w1, ffn2_w3, ffn2_w2):
    S = x.shape[1]
    ROWS = S // GRID_W
    f32 = jnp.float32
    rows = jnp.repeat(jnp.arange(ROWS, dtype=f32), GRID_W)
    cols = jnp.tile(jnp.arange(GRID_W, dtype=f32), ROWS)
    inv = ROPE_BASE ** (-jnp.arange(0, AXIS_DIM, 2, dtype=f32) / AXIS_DIM)
    ang_r = rows[:, None] * inv
    ang_c = cols[:, None] * inv
    ang = jnp.concatenate([ang_r, ang_r, ang_c, ang_c], axis=-1)
    cos = jnp.cos(ang).astype(x.dtype)
    sin = jnp.sin(ang).astype(x.dtype)

    layer_weights = (w_ada, b_ada, norm1_w, ffn1_w1, ffn1_w3, ffn1_w2,
                     norm2_w, w_in, q_a_norm_w, w_uq, kv_a_norm_w, w_ukv, q_norm_w, k_norm_w,
                     v_norm_w, w_s, b_s, w_out, norm3_w, ffn2_w1, ffn2_w3, ffn2_w2)
    for i in range(DEPTH):
        x, ctx = hybrid_layer(x, ctx, c, c_ctx, cos, sin, *[w[i] for w in layer_weights],
                              update_ctx=i < DEPTH - 1)
    return x
```

```python
import functools

import numpy as np
import jax
import jax.numpy as jnp
from jax import lax
from jax.experimental import pallas as pl
from jax.experimental.pallas import tpu as pltpu

D_MODEL = 1024
GRID_W = 64
MLA_HEADS = 8
QK_NOPE = 64
QK_ROPE = 32
QK_HEAD = QK_NOPE + QK_ROPE
V_HEAD = 64
Q_LORA = 256
KV_LORA = 128
AXIS_DIM = QK_ROPE // 2
ROPE_BASE = 10000.0
GMLP_GROUPS = 8
GMLP_GROUP_DIM = 64
GMLP_WIDTH = GMLP_GROUPS * GMLP_GROUP_DIM
CHUNK = 128
KV_COLS = KV_LORA + QK_ROPE
Q_START = KV_COLS
U_START = KV_COLS + Q_LORA
V_START = U_START + GMLP_WIDTH
IN_COLS = V_START + GMLP_WIDTH
D_FF = 2816
N_MOD = 9
EPS = 1e-6

LANES = 128
HEAD_PAD = LANES
N_PAIRS = MLA_HEADS // 2
VMEM_LIMIT = 56 * 1024 * 1024

EXT_KV = 0
EXT_A = 128
EXT_B = 256
EXT_Q = 384
EXT_U = EXT_Q + Q_LORA
EXT_V = EXT_U + GMLP_WIDTH
EXT_COLS = EXT_V + GMLP_WIDTH
EXT_KV_ONLY = EXT_Q

FF_TILE = 256
N_FF = D_FF // FF_TILE


def _rot_perm():
    half = AXIS_DIM // 2
    perm = np.zeros(QK_ROPE, np.int32)
    sign = np.zeros(QK_ROPE, np.float32)
    for l in range(QK_ROPE):
        if (l // half) % 2 == 0:
            perm[l], sign[l] = l + half, -1.0
        else:
            perm[l], sign[l] = l - half, 1.0
    return perm, sign


_ROT_PERM, _ROT_SIGN = _rot_perm()


def _rms_scale(x, n):
    return lax.rsqrt(jnp.sum(x * x, axis=-1, keepdims=True) * (1.0 / n) + EPS)


def _silu(a):
    return a / (1.0 + jnp.exp(-a))


def _gelu_tanh(x):
    c = np.float32(np.sqrt(2.0 / np.pi))
    return 0.5 * x * (1.0 + jnp.tanh(c * (x + 0.044715 * (x * x * x))))


def _ada_kernel(c_ref, w_ref, b_ref, o_ref):
    s = _silu(c_ref[...]).astype(jnp.bfloat16)
    o_ref[...] = jnp.dot(s, w_ref[...].astype(jnp.bfloat16),
                         preferred_element_type=jnp.float32) + b_ref[...]


def _ada_call(cc, w_ada, b_ada):
    rows = cc.shape[0]
    n = w_ada.shape[1]
    tn = 1024
    return pl.pallas_call(
        _ada_kernel,
        grid=(n // tn,),
        in_specs=[pl.BlockSpec((rows, D_MODEL), lambda j: (0, 0)),
                  pl.BlockSpec((D_MODEL, tn), lambda j: (0, j)),
                  pl.BlockSpec((1, tn), lambda j: (0, j))],
        out_specs=pl.BlockSpec((rows, tn), lambda j: (0, j)),
        out_shape=jax.ShapeDtypeStruct((rows, n), jnp.float32),
        compiler_params=pltpu.CompilerParams(dimension_semantics=("arbitrary",),
                                             vmem_limit_bytes=VMEM_LIMIT),
        name="adaln",
    )(cc, w_ada, b_ada)


def _ffn_core(x, mod_ref, nw_ref, w1_ref, w3_ref, w2_ref, h_ref, acc_ref):
    shift, scale, gate = mod_ref[0, 0], mod_ref[0, 1], mod_ref[0, 2]
    h = x * _rms_scale(x, D_MODEL) * nw_ref[...]
    h_ref[...] = (h * (1.0 + scale) + shift).astype(jnp.bfloat16)
    acc_ref[...] = jnp.zeros_like(acc_ref)

    def body(j, carry):
        hb = h_ref[...]
        cols = pl.ds(pl.multiple_of(j * FF_TILE, FF_TILE), FF_TILE)
        a = jnp.dot(hb, w1_ref[:, cols], preferred_element_type=jnp.float32)
        b = jnp.dot(hb, w3_ref[:, cols], preferred_element_type=jnp.float32)
        g = (_silu(a) * b).astype(jnp.bfloat16)
        acc_ref[...] += jnp.dot(g, w2_ref[j], preferred_element_type=jnp.float32)
        return carry

    lax.fori_loop(0, N_FF, body, 0)
    return x + (0.5 * gate) * acc_ref[...]


def _ffn_kernel(x_ref, mod_ref, nw_ref, w1_ref, w3_ref, w2_ref, o_ref, h_ref, acc_ref):
    o_ref[0] = _ffn_core(x_ref[0], mod_ref, nw_ref, w1_ref, w3_ref, w2_ref, h_ref, acc_ref)


def _const_spec(shape):
    nd = len(shape)
    return pl.BlockSpec(shape, lambda *_: (0,) * nd, pipeline_mode=pl.Buffered(1))


def _mod_spec(mod_block, per_batch):
    if per_batch:
        return pl.BlockSpec((1, 3, 1, D_MODEL), lambda b, i: (b, mod_block, 0, 0))
    return pl.BlockSpec((1, 3, 1, D_MODEL), lambda b, i: (0, mod_block, 0, 0))


def _ffn_call(x, mod, mod_block, per_batch, norm_w, w1, w3, w2, tm):
    bsz, seq, _ = x.shape
    return pl.pallas_call(
        _ffn_kernel,
        grid=(bsz, seq // tm),
        in_specs=[pl.BlockSpec((1, tm, D_MODEL), lambda b, i: (b, i, 0)),
                  _mod_spec(mod_block, per_batch),
                  _const_spec((1, D_MODEL)),
                  _const_spec((D_MODEL, D_FF)),
                  _const_spec((D_MODEL, D_FF)),
                  _const_spec((N_FF, FF_TILE, D_MODEL))],
        out_specs=pl.BlockSpec((1, tm, D_MODEL), lambda b, i: (b, i, 0)),
        out_shape=jax.ShapeDtypeStruct(x.shape, jnp.float32),
        scratch_shapes=[pltpu.VMEM((tm, D_MODEL), jnp.bfloat16),
                        pltpu.VMEM((tm, D_MODEL), jnp.float32)],
        compiler_params=pltpu.CompilerParams(dimension_semantics=("arbitrary", "arbitrary"),
                                             vmem_limit_bytes=VMEM_LIMIT),
        name="ffn",
    )(x, mod, norm_w, w1, w3, w2)


def _lane_iota(shape):
    return lax.broadcasted_iota(jnp.int32, shape, len(shape) - 1)


def _kv_prep(proj, kvn_ref, wukv_ref, wkn_ref, wka_ref, wkb_ref, tkc_ref, tks_ref, k_ref, v_ref):
    tm = proj.shape[0]
    ckv = proj[:, EXT_KV:EXT_KV + KV_LORA]
    ckv = (ckv * _rms_scale(ckv, KV_LORA) * kvn_ref[...]).astype(jnp.bfloat16)
    kv = jnp.dot(ckv, wukv_ref[...], preferred_element_type=jnp.float32)
    v_ref[0] = kv[:, MLA_HEADS * QK_NOPE:].astype(jnp.bfloat16)
    a = proj[:, EXT_A:EXT_A + LANES]
    b = proj[:, EXT_B:EXT_B + LANES]
    ss_pe = 0.5 * jnp.sum(a * a, axis=-1, keepdims=True)
    p_even = a * (wka_ref[...] * tkc_ref[...]) + b * (wkb_ref[...] * tks_ref[...])
    p_odd = pltpu.roll(p_even, 64, axis=1)
    lane = _lane_iota((tm, LANES))
    lo = lane < 64
    for p in range(N_PAIRS):
        kp = kv[:, p * LANES:(p + 1) * LANES]
        sq = kp * kp
        ss_e = jnp.sum(jnp.where(lo, sq, 0.0), axis=-1, keepdims=True)
        ss_o = jnp.sum(jnp.where(lo, 0.0, sq), axis=-1, keepdims=True)
        r_e = lax.rsqrt((ss_e + ss_pe) * (1.0 / QK_HEAD) + EPS)
        r_o = lax.rsqrt((ss_o + ss_pe) * (1.0 / QK_HEAD) + EPS)
        kw = kp * wkn_ref[...]
        k_ref[0, 2 * p] = ((jnp.where(lo, kw, 0.0) + p_even) * r_e).astype(jnp.bfloat16)
        k_ref[0, 2 * p + 1] = ((jnp.where(lo, 0.0, kw) + p_odd) * r_o).astype(jnp.bfloat16)


def _modulated_proj(x, mod_ref, nw_ref, win_ref):
    shift, scale = mod_ref[0, 0], mod_ref[0, 1]
    h = x * _rms_scale(x, D_MODEL) * nw_ref[...]
    h = (h * (1.0 + scale) + shift).astype(jnp.bfloat16)
    return jnp.dot(h, win_ref[...], preferred_element_type=jnp.float32)


def _prep_kernel(x_ref, mod_ref, nw_ref, win_ref, kvn_ref, wukv_ref, wkn_ref, wka_ref, wkb_ref,
                 tkc_ref, tks_ref, qan_ref, wuq_ref, wqn_ref, tq_ref, vnw_ref, ws_ref, bs_ref,
                 k_ref, v_ref, q_ref, sg_ref):
    tm = x_ref.shape[1]
    proj = _modulated_proj(x_ref[0], mod_ref, nw_ref, win_ref)
    _kv_prep(proj, kvn_ref, wukv_ref, wkn_ref, wka_ref, wkb_ref, tkc_ref, tks_ref, k_ref, v_ref)

    cq = proj[:, EXT_Q:EXT_Q + Q_LORA]
    cq = (cq * _rms_scale(cq, Q_LORA) * qan_ref[...]).astype(jnp.bfloat16)
    qall = jnp.dot(cq, wuq_ref[...], preferred_element_type=jnp.float32)
    lane = _lane_iota((tm, LANES))
    real = (lane < QK_HEAD, jnp.logical_or(lane < QK_ROPE, lane >= 2 * QK_ROPE))
    for h in range(MLA_HEADS):
        par = h % 2
        qh = qall[:, h * LANES:(h + 1) * LANES]
        ss = jnp.sum(jnp.where(real[par], qh * qh, 0.0), axis=-1, keepdims=True)
        r = lax.rsqrt(ss * (1.0 / QK_HEAD) + EPS) * (QK_HEAD ** -0.5)
        tab = wqn_ref[:, par * LANES:(par + 1) * LANES] * tq_ref[:, par * LANES:(par + 1) * LANES]
        q_ref[0, h] = (qh * r * tab).astype(jnp.bfloat16)

    u = _gelu_tanh(proj[:, EXT_U:EXT_U + GMLP_WIDTH])
    v = _gelu_tanh(proj[:, EXT_V:EXT_V + GMLP_WIDTH])
    lo = lane < 64
    vn_tiles = []
    for p in range(GMLP_GROUPS // 2):
        vp = v[:, p * LANES:(p + 1) * LANES]
        sq = vp * vp
        r_lo = lax.rsqrt(jnp.sum(jnp.where(lo, sq, 0.0), axis=-1, keepdims=True)
                         * (1.0 / GMLP_GROUP_DIM) + EPS)
        r_hi = lax.rsqrt(jnp.sum(jnp.where(lo, 0.0, sq), axis=-1, keepdims=True)
                         * (1.0 / GMLP_GROUP_DIM) + EPS)
        vn = vp * jnp.where(lo, r_lo, r_hi) * vnw_ref[:, p * LANES:(p + 1) * LANES]
        vn_tiles.append(vn.astype(jnp.bfloat16))
    lo_c = _lane_iota((CHUNK, LANES)) < 64
    for c in range(tm // CHUNK):
        rows = slice(c * CHUNK, (c + 1) * CHUNK)
        for p in range(GMLP_GROUPS // 2):
            o = jnp.dot(ws_ref[p], vn_tiles[p][rows], preferred_element_type=jnp.float32)
            s = jnp.where(lo_c, o[:CHUNK], o[CHUNK:]) + bs_ref[:, p * LANES:(p + 1) * LANES]
            sg_ref[0, rows, p * LANES:(p + 1) * LANES] = (
                u[rows, p * LANES:(p + 1) * LANES] * s).astype(jnp.bfloat16)


def _kvonly_kernel(x_ref, mod_ref, nw_ref, win_ref, kvn_ref, wukv_ref, wkn_ref, wka_ref, wkb_ref,
                   tkc_ref, tks_ref, k_ref, v_ref):
    proj = _modulated_proj(x_ref[0], mod_ref, nw_ref, win_ref)
    _kv_prep(proj, kvn_ref, wukv_ref, wkn_ref, wka_ref, wkb_ref, tkc_ref, tks_ref, k_ref, v_ref)


def _prep_call(x, mod, per_batch, wts, tabs, tm, with_q):
    bsz, seq, _ = x.shape
    x_spec = pl.BlockSpec((1, tm, D_MODEL), lambda b, i: (b, i, 0))
    tab_spec = lambda w: pl.BlockSpec((tm, w), lambda b, i: (i, 0))
    ncols = EXT_COLS if with_q else EXT_KV_ONLY
    kv_specs = [x_spec, _mod_spec(1, per_batch), _const_spec((1, D_MODEL)),
                _const_spec((D_MODEL, ncols)), _const_spec((1, KV_LORA)),
                _const_spec((KV_LORA, MLA_HEADS * LANES)), _const_spec((1, LANES)),
                _const_spec((1, LANES)), _const_spec((1, LANES)), tab_spec(LANES), tab_spec(LANES)]
    kv_args = [x, mod, wts["norm2"], wts["w_in"] if with_q else wts["w_in"][:, :EXT_KV_ONLY],
               wts["kvn"], wts["w_ukv"], wts["wk_nope"], wts["wk_a"], wts["wk_b"],
               tabs["kc"], tabs["ks"]]
    k_shape = jax.ShapeDtypeStruct((bsz, MLA_HEADS, seq, HEAD_PAD), jnp.bfloat16)
    v_shape = jax.ShapeDtypeStruct((bsz, seq, MLA_HEADS * V_HEAD), jnp.bfloat16)
    k_spec = pl.BlockSpec((1, MLA_HEADS, tm, HEAD_PAD), lambda b, i: (b, 0, i, 0))
    v_spec = pl.BlockSpec((1, tm, MLA_HEADS * V_HEAD), lambda b, i: (b, i, 0))
    params = pltpu.CompilerParams(dimension_semantics=("arbitrary", "arbitrary"),
                                  vmem_limit_bytes=VMEM_LIMIT)
    if not with_q:
        return pl.pallas_call(
            _kvonly_kernel, grid=(bsz, seq // tm), in_specs=kv_specs,
            out_specs=[k_spec, v_spec], out_shape=[k_shape, v_shape],
            compiler_params=params, name="kv_prep")(*kv_args)
    q_specs = [_const_spec((1, Q_LORA)), _const_spec((Q_LORA, MLA_HEADS * LANES)),
               _const_spec((1, 2 * LANES)), tab_spec(2 * LANES), _const_spec((1, GMLP_WIDTH)),
               _const_spec((GMLP_GROUPS // 2, 2 * CHUNK, CHUNK)), _const_spec((CHUNK, GMLP_WIDTH))]
    q_args = [wts["qan"], wts["w_uq"], wts["wq"], tabs["q"], wts["vnw"], wts["ws"], wts["bs"]]
    return pl.pallas_call(
        _prep_kernel, grid=(bsz, seq // tm), in_specs=kv_specs + q_specs,
        out_specs=[k_spec, v_spec, k_spec, v_spec],
        out_shape=[k_shape, v_shape, k_shape, v_shape],
        compiler_params=params, name="mix_prep")(*kv_args, *q_args)


def _attn_kernel(q_ref, kl_ref, kc_ref, vl_ref, vc_ref, o_ref):
    nt = (((1,), (1,)), ((), ()))
    outs = []
    for h in range(2):
        q = q_ref[0, h]
        s1 = lax.dot_general(q, kl_ref[0, h], nt, preferred_element_type=jnp.float32)
        s2 = lax.dot_general(q, kc_ref[0, h], nt, preferred_element_type=jnp.float32)
        m = jnp.maximum(jnp.max(s1, axis=-1, keepdims=True), jnp.max(s2, axis=-1, keepdims=True))
        p1 = jnp.exp(s1 - m)
        p2 = jnp.exp(s2 - m)
        l = jnp.sum(p1, axis=-1, keepdims=True) + jnp.sum(p2, axis=-1, keepdims=True)
        o = (jnp.dot(p1.astype(jnp.bfloat16), vl_ref[0], preferred_element_type=jnp.float32)
             + jnp.dot(p2.astype(jnp.bfloat16), vc_ref[0], preferred_element_type=jnp.float32))
        outs.append(o / l)
    lo = _lane_iota(outs[0].shape) < 64
    o_ref[0] = jnp.where(lo, outs[0], outs[1]).astype(jnp.bfloat16)


def _attn_call(q, k_lat, k_ctx, v_lat, v_ctx, tq):
    bsz, _, seq, _ = q.shape
    n_ctx = k_ctx.shape[2]
    return pl.pallas_call(
        _attn_kernel,
        grid=(bsz, N_PAIRS, seq // tq),
        in_specs=[pl.BlockSpec((1, 2, tq, HEAD_PAD), lambda b, p, i: (b, p, i, 0)),
                  pl.BlockSpec((1, 2, seq, HEAD_PAD), lambda b, p, i: (b, p, 0, 0)),
                  pl.BlockSpec((1, 2, n_ctx, HEAD_PAD), lambda b, p, i: (b, p, 0, 0)),
                  pl.BlockSpec((1, seq, LANES), lambda b, p, i: (b, 0, p)),
                  pl.BlockSpec((1, n_ctx, LANES), lambda b, p, i: (b, 0, p))],
        out_specs=pl.BlockSpec((1, tq, LANES), lambda b, p, i: (b, i, p)),
        out_shape=jax.ShapeDtypeStruct((bsz, seq, MLA_HEADS * V_HEAD), jnp.bfloat16),
        compiler_params=pltpu.CompilerParams(
            dimension_semantics=("arbitrary", "arbitrary", "arbitrary"),
            vmem_limit_bytes=VMEM_LIMIT),
        name="attention",
    )(q, k_lat, k_ctx, v_lat, v_ctx)


def _out_ffn_kernel(x_ref, attn_ref, sg_ref, modm_ref, wout_ref, mod_ref, nw_ref,
                    w1_ref, w3_ref, w2_ref, o_ref, h_ref, acc_ref):
    y = (jnp.dot(attn_ref[0], wout_ref[0], preferred_element_type=jnp.float32)
         + jnp.dot(sg_ref[0], wout_ref[1], preferred_element_type=jnp.float32))
    x = x_ref[0] + modm_ref[0, 2] * y
    o_ref[0] = _ffn_core(x, mod_ref, nw_ref, w1_ref, w3_ref, w2_ref, h_ref, acc_ref)


def _out_ffn_call(x, attn, sg, mod, w_out, norm_w, w1, w3, w2, tm):
    bsz, seq, _ = x.shape
    half = MLA_HEADS * V_HEAD
    row = lambda w: pl.BlockSpec((1, tm, w), lambda b, i: (b, i, 0))
    return pl.pallas_call(
        _out_ffn_kernel,
        grid=(bsz, seq // tm),
        in_specs=[row(D_MODEL), row(half), row(GMLP_WIDTH), _mod_spec(1, True),
                  _const_spec((2, half, D_MODEL)), _mod_spec(2, True), _const_spec((1, D_MODEL)),
                  _const_spec((D_MODEL, D_FF)), _const_spec((D_MODEL, D_FF)),
                  _const_spec((N_FF, FF_TILE, D_MODEL))],
        out_specs=row(D_MODEL),
        out_shape=jax.ShapeDtypeStruct(x.shape, jnp.float32),
        scratch_shapes=[pltpu.VMEM((tm, D_MODEL), jnp.bfloat16),
                        pltpu.VMEM((tm, D_MODEL), jnp.float32)],
        compiler_params=pltpu.CompilerParams(dimension_semantics=("arbitrary", "arbitrary"),
                                             vmem_limit_bytes=VMEM_LIMIT),
        name="out_ffn",
    )(x, attn, sg, mod, w_out, mod, norm_w, w1, w3, w2)


def _ffn_weights(w1, w3, w2):
    bf = jnp.bfloat16
    return w1.astype(bf), w3.astype(bf), w2.astype(bf).reshape(N_FF, FF_TILE, D_MODEL)


def _rot_cols(w, start, signed=True):
    half = AXIS_DIM // 2
    parts = []
    for blk in range(QK_ROPE // half):
        src = start + (blk + 1) * half if blk % 2 == 0 else start + (blk - 1) * half
        piece = w[..., src:src + half]
        parts.append(-piece if (signed and blk % 2 == 0) else piece)
    return jnp.concatenate(parts, axis=-1)


def _mix_weights(norm2_w, w_in, q_a_norm_w, w_uq, kv_a_norm_w, w_ukv, q_norm_w, k_norm_w,
                 v_norm_w, w_s, b_s):
    bf = jnp.bfloat16
    z_in = jnp.zeros((D_MODEL, 64), jnp.float32)
    kpe = w_in[:, KV_LORA:KV_COLS]
    kpe_rot = _rot_cols(w_in, KV_LORA)
    w_in_ext = jnp.concatenate([w_in[:, :KV_LORA], z_in, kpe, kpe, z_in, kpe_rot, kpe_rot,
                                w_in[:, Q_START:]], axis=1).astype(bf)
    ukv = w_ukv.reshape(KV_LORA, MLA_HEADS, QK_NOPE + V_HEAD)
    w_ukv_p = jnp.concatenate([ukv[:, :, :QK_NOPE].reshape(KV_LORA, -1),
                               ukv[:, :, QK_NOPE:].reshape(KV_LORA, -1)], axis=1).astype(bf)
    cols = []
    for h in range(MLA_HEADS):
        base = h * QK_HEAD
        nope = w_uq[:, base:base + QK_NOPE]
        pes = w_uq[:, base + QK_NOPE:base + QK_HEAD]
        rots = _rot_cols(w_uq, base + QK_NOPE)
        cols += [nope, pes, rots] if h % 2 == 0 else [pes, rots, nope]
    w_uq_ext = jnp.concatenate(cols, axis=1).astype(bf)
    qn, qp, qr = q_norm_w[:QK_NOPE], q_norm_w[QK_NOPE:], _rot_cols(q_norm_w, QK_NOPE, signed=False)
    wq = jnp.concatenate([qn, qp, qr, qp, qr, qn])[None]
    kn, kp, kr = k_norm_w[:QK_NOPE], k_norm_w[QK_NOPE:], _rot_cols(k_norm_w, QK_NOPE, signed=False)
    z64 = jnp.zeros((64,), jnp.float32)
    return dict(
        norm2=norm2_w[None], w_in=w_in_ext, kvn=kv_a_norm_w[None], w_ukv=w_ukv_p,
        wk_nope=jnp.concatenate([kn, kn])[None],
        wk_a=jnp.concatenate([z64, kp, kp])[None], wk_b=jnp.concatenate([z64, kr, kr])[None],
        qan=q_a_norm_w[None], w_uq=w_uq_ext, wq=wq,
        vnw=v_norm_w.reshape(1, GMLP_WIDTH),
        ws=w_s.astype(bf).reshape(GMLP_GROUPS // 2, 2 * CHUNK, CHUNK),
        bs=jnp.broadcast_to(b_s.T[:, :, None], (CHUNK, GMLP_GROUPS, GMLP_GROUP_DIM)
                            ).reshape(CHUNK, GMLP_WIDTH),
    )


def _rope_tables(seq, n_ctx):
    f32 = jnp.float32
    rows_n = seq // GRID_W
    rows = jnp.repeat(jnp.arange(rows_n, dtype=f32), GRID_W)
    cols = jnp.tile(jnp.arange(GRID_W, dtype=f32), rows_n)
    inv = ROPE_BASE ** (-jnp.arange(0, AXIS_DIM, 2, dtype=f32) / AXIS_DIM)
    ang_r = rows[:, None] * inv
    ang_c = cols[:, None] * inv
    ang = jnp.concatenate([ang_r, ang_r, ang_c, ang_c], axis=-1)
    cos, sin = jnp.cos(ang), jnp.sin(ang)
    one = jnp.ones((seq, 64), f32)
    zero = jnp.zeros((seq, 64), f32)
    lat = dict(kc=jnp.concatenate([zero, cos, cos], 1), ks=jnp.concatenate([zero, sin, sin], 1),
               q=jnp.concatenate([one, cos, sin, cos, sin, one], 1))
    ctx = dict(kc=jnp.concatenate([jnp.zeros((n_ctx, 64), f32), jnp.ones((n_ctx, 64), f32)], 1),
               ks=jnp.zeros((n_ctx, LANES), f32))
    return lat, ctx


def kernel(x, c, ctx, c_ctx, w_ada, b_ada, norm1_w, ffn1_w1, ffn1_w3, ffn1_w2, norm2_w, w_in,
           q_a_norm_w, w_uq, kv_a_norm_w, w_ukv, q_norm_w, k_norm_w, v_norm_w, w_s, b_s, w_out,
           norm3_w, ffn2_w1, ffn2_w3, ffn2_w2):
    bsz, seq, _ = x.shape
    n_ctx = ctx.shape[1]
    rows = -(-(bsz + 1) // 8) * 8
    cc = jnp.concatenate([c, c_ctx[None], jnp.zeros((rows - bsz - 1, D_MODEL), jnp.float32)], 0)
    mod = _ada_call(cc, w_ada[0], b_ada[0][None]).reshape(rows, N_MOD, 1, D_MODEL)
    mod_ctx = mod[bsz:bsz + 1]

    f1 = _ffn_weights(ffn1_w1[0], ffn1_w3[0], ffn1_w2[0])
    f2 = _ffn_weights(ffn2_w1[0], ffn2_w3[0], ffn2_w2[0])
    wts = _mix_weights(norm2_w[0], w_in[0], q_a_norm_w[0], w_uq[0], kv_a_norm_w[0], w_ukv[0],
                       q_norm_w[0], k_norm_w[0], v_norm_w[0], w_s[0], b_s[0])
    tabs_lat, tabs_ctx = _rope_tables(seq, n_ctx)

    x1 = _ffn_call(x, mod, 0, True, norm1_w, *f1, tm=512)
    ctx1 = _ffn_call(ctx, mod_ctx, 0, False, norm1_w, *f1, tm=n_ctx)
    k_lat, v_lat, q, sg = _prep_call(x1, mod, True, wts, tabs_lat, tm=512, with_q=True)
    k_ctx, v_ctx = _prep_call(ctx1, mod_ctx, False, wts, tabs_ctx, tm=n_ctx, with_q=False)
    attn = _attn_call(q, k_lat, k_ctx, v_lat, v_ctx, tq=256)
    w_out_r = w_out[0].astype(jnp.bfloat16).reshape(2, MLA_HEADS * V_HEAD, D_MODEL)
    return _out_ffn_call(x1, attn, sg, mod, w_out_r, norm3_w, *f2, tm=512)
```

```python
import functools

import numpy as np
import jax
import jax.numpy as jnp
from jax import lax
from jax.experimental import pallas as pl
from jax.experimental.pallas import tpu as pltpu

D_MODEL = 1024
GRID_W = 64
MLA_HEADS = 8
QK_NOPE = 64
QK_ROPE = 32
QK_HEAD = QK_NOPE + QK_ROPE
V_HEAD = 64
Q_LORA = 256
KV_LORA = 128
AXIS_DIM = QK_ROPE // 2
ROPE_BASE = 10000.0
GMLP_GROUPS = 8
GMLP_GROUP_DIM = 64
GMLP_WIDTH = GMLP_GROUPS * GMLP_GROUP_DIM
CHUNK = 128
KV_COLS = KV_LORA + QK_ROPE
Q_START = KV_COLS
U_START = KV_COLS + Q_LORA
V_START = U_START + GMLP_WIDTH
IN_COLS = V_START + GMLP_WIDTH
D_FF = 2816
N_MOD = 9
EPS = 1e-6

LANES = 128
HEAD_PAD = LANES
N_PAIRS = MLA_HEADS // 2
VMEM_LIMIT = 56 * 1024 * 1024

EXT_KV = 0
EXT_A = 128
EXT_B = 256
EXT_Q = 384
EXT_U = EXT_Q + Q_LORA
EXT_V = EXT_U + GMLP_WIDTH
EXT_COLS = EXT_V + GMLP_WIDTH
EXT_KV_ONLY = EXT_Q

FF_TILE = 256
N_FF = D_FF // FF_TILE


def _rms_scale(x, n):
    return lax.rsqrt(jnp.sum(x * x, axis=-1, keepdims=True) * (1.0 / n) + EPS)


def _silu(a):
    return a / (1.0 + jnp.exp(-a))


def _gelu_tanh(x):
    c = np.float32(np.sqrt(2.0 / np.pi))
    return 0.5 * x * (1.0 + jnp.tanh(c * (x + 0.044715 * (x * x * x))))


def _ada_kernel(c_ref, w_ref, b_ref, o_ref):
    s = _silu(c_ref[...]).astype(jnp.bfloat16)
    o_ref[...] = jnp.dot(s, w_ref[...].astype(jnp.bfloat16),
                         preferred_element_type=jnp.float32) + b_ref[...]


def _ada_call(cc, w_ada, b_ada):
    rows = cc.shape[0]
    n = w_ada.shape[1]
    tn = 1024
    return pl.pallas_call(
        _ada_kernel,
        grid=(n // tn,),
        in_specs=[pl.BlockSpec((rows, D_MODEL), lambda j: (0, 0)),
                  pl.BlockSpec((D_MODEL, tn), lambda j: (0, j)),
                  pl.BlockSpec((1, tn), lambda j: (0, j))],
        out_specs=pl.BlockSpec((rows, tn), lambda j: (0, j)),
        out_shape=jax.ShapeDtypeStruct((rows, n), jnp.float32),
        compiler_params=pltpu.CompilerParams(dimension_semantics=("arbitrary",),
                                             vmem_limit_bytes=VMEM_LIMIT),
        name="adaln",
    )(cc, w_ada, b_ada)


def _ffn_core(x, mod_ref, nw_ref, w1_ref, w3_ref, w2_ref, h_ref, acc_ref):
    shift, scale, gate = mod_ref[0, 0], mod_ref[0, 1], mod_ref[0, 2]
    h = x * _rms_scale(x, D_MODEL) * nw_ref[...]
    h_ref[...] = (h * (1.0 + scale) + shift).astype(jnp.bfloat16)
    acc_ref[...] = jnp.zeros_like(acc_ref)

    for j in range(N_FF):
        hb = h_ref[...]
        cols = slice(j * FF_TILE, (j + 1) * FF_TILE)
        a = jnp.dot(hb, w1_ref[:, cols], preferred_element_type=jnp.float32)
        b = jnp.dot(hb, w3_ref[:, cols], preferred_element_type=jnp.float32)
        g = (_silu(a) * b).astype(jnp.bfloat16)
        acc_ref[...] += jnp.dot(g, w2_ref[j], preferred_element_type=jnp.float32)
    return x + (0.5 * gate) * acc_ref[...]


def _ffn_kernel(x_ref, mod_ref, nw_ref, w1_ref, w3_ref, w2_ref, o_ref, h_ref, acc_ref):
    o_ref[0] = _ffn_core(x_ref[0], mod_ref, nw_ref, w1_ref, w3_ref, w2_ref, h_ref, acc_ref)


def _const_spec(shape):
    nd = len(shape)
    return pl.BlockSpec(shape, lambda *_: (0,) * nd, pipeline_mode=pl.Buffered(1))


def _mod_spec(mod_block, per_batch):
    if per_batch:
        return pl.BlockSpec((1, 3, 1, D_MODEL), lambda b, i: (b, mod_block, 0, 0))
    return pl.BlockSpec((1, 3, 1, D_MODEL), lambda b, i: (0, mod_block, 0, 0))


def _ffn_call(x, mod, mod_block, per_batch, norm_w, w1, w3, w2, tm):
    bsz, seq, _ = x.shape
    return pl.pallas_call(
        _ffn_kernel,
        grid=(bsz, seq // tm),
        in_specs=[pl.BlockSpec((1, tm, D_MODEL), lambda b, i: (b, i, 0)),
                  _mod_spec(mod_block, per_batch),
                  _const_spec((1, D_MODEL)),
                  _const_spec((D_MODEL, D_FF)),
                  _const_spec((D_MODEL, D_FF)),
                  _const_spec((N_FF, FF_TILE, D_MODEL))],
        out_specs=pl.BlockSpec((1, tm, D_MODEL), lambda b, i: (b, i, 0)),
        out_shape=jax.ShapeDtypeStruct(x.shape, jnp.float32),
        scratch_shapes=[pltpu.VMEM((tm, D_MODEL), jnp.bfloat16),
                        pltpu.VMEM((tm, D_MODEL), jnp.float32)],
        compiler_params=pltpu.CompilerParams(dimension_semantics=("arbitrary", "arbitrary"),
                                             vmem_limit_bytes=VMEM_LIMIT),
        name="ffn",
    )(x, mod, norm_w, w1, w3, w2)


def _lane_iota(shape):
    return lax.broadcasted_iota(jnp.int32, shape, len(shape) - 1)


def _kv_prep(proj, kvn_ref, wukv_ref, wkn_ref, wka_ref, wkb_ref, tkc_ref, tks_ref, k_ref, v_ref):
    tm = proj.shape[0]
    ckv = proj[:, EXT_KV:EXT_KV + KV_LORA]
    ckv = (ckv * _rms_scale(ckv, KV_LORA) * kvn_ref[...]).astype(jnp.bfloat16)
    kv = jnp.dot(ckv, wukv_ref[...], preferred_element_type=jnp.float32)
    a = proj[:, EXT_A:EXT_A + LANES]
    b = proj[:, EXT_B:EXT_B + LANES]
    ss_pe = 0.5 * jnp.sum(a * a, axis=-1, keepdims=True)
    p_even = a * (wka_ref[...] * tkc_ref[...]) + b * (wkb_ref[...] * tks_ref[...])
    p_odd = pltpu.roll(p_even, 64, axis=1)
    lane = _lane_iota((tm, LANES))
    lo = lane < 64
    one_e = jnp.where(lane == V_HEAD, 1.0, 0.0)
    one_o = jnp.where(lane == 0, 1.0, 0.0)
    for p in range(N_PAIRS):
        vp = kv[:, MLA_HEADS * QK_NOPE + p * LANES:MLA_HEADS * QK_NOPE + (p + 1) * LANES]
        v_ref[0, :, (2 * p) * LANES:(2 * p + 1) * LANES] = jnp.where(lo, vp, one_e).astype(jnp.bfloat16)
        v_ref[0, :, (2 * p + 1) * LANES:(2 * p + 2) * LANES] = jnp.where(lo, one_o, vp).astype(jnp.bfloat16)
        kp = kv[:, p * LANES:(p + 1) * LANES]
        sq = kp * kp
        ss_e = jnp.sum(jnp.where(lo, sq, 0.0), axis=-1, keepdims=True)
        ss_o = jnp.sum(jnp.where(lo, 0.0, sq), axis=-1, keepdims=True)
        r_e = lax.rsqrt((ss_e + ss_pe) * (1.0 / QK_HEAD) + EPS)
        r_o = lax.rsqrt((ss_o + ss_pe) * (1.0 / QK_HEAD) + EPS)
        kw = kp * wkn_ref[...]
        k_ref[0, 2 * p] = ((jnp.where(lo, kw, 0.0) + p_even) * r_e).astype(jnp.bfloat16)
        k_ref[0, 2 * p + 1] = ((jnp.where(lo, 0.0, kw) + p_odd) * r_o).astype(jnp.bfloat16)


def _modulated_proj(x, mod_ref, nw_ref, win_ref):
    shift, scale = mod_ref[0, 0], mod_ref[0, 1]
    h = x * _rms_scale(x, D_MODEL) * nw_ref[...]
    h = (h * (1.0 + scale) + shift).astype(jnp.bfloat16)
    return jnp.dot(h, win_ref[...], preferred_element_type=jnp.float32)


def _prep_kernel(x_ref, mod_ref, nw_ref, win_ref, kvn_ref, wukv_ref, wkn_ref, wka_ref, wkb_ref,
                 tkc_ref, tks_ref, qan_ref, wuq_ref, wqn_ref, tq_ref, vnw_ref, ws_ref, bs_ref,
                 k_ref, v_ref, q_ref, sg_ref):
    tm = x_ref.shape[1]
    proj = _modulated_proj(x_ref[0], mod_ref, nw_ref, win_ref)
    _kv_prep(proj, kvn_ref, wukv_ref, wkn_ref, wka_ref, wkb_ref, tkc_ref, tks_ref, k_ref, v_ref)

    cq = proj[:, EXT_Q:EXT_Q + Q_LORA]
    cq = (cq * _rms_scale(cq, Q_LORA) * qan_ref[...]).astype(jnp.bfloat16)
    qall = jnp.dot(cq, wuq_ref[...], preferred_element_type=jnp.float32)
    lane = _lane_iota((tm, LANES))
    real = (lane < QK_HEAD, jnp.logical_or(lane < QK_ROPE, lane >= 2 * QK_ROPE))
    tabs = [wqn_ref[:, par * LANES:(par + 1) * LANES] * tq_ref[:, par * LANES:(par + 1) * LANES]
            * np.float32(QK_HEAD ** -0.5 * np.log2(np.e)) for par in range(2)]
    for h in range(MLA_HEADS):
        par = h % 2
        qh = qall[:, h * LANES:(h + 1) * LANES]
        ss = jnp.sum(jnp.where(real[par], qh * qh, 0.0), axis=-1, keepdims=True)
        r = lax.rsqrt(ss * (1.0 / QK_HEAD) + EPS)
        q_ref[0, h] = (qh * r * tabs[par]).astype(jnp.bfloat16)

    u = _gelu_tanh(proj[:, EXT_U:EXT_U + GMLP_WIDTH])
    v = _gelu_tanh(proj[:, EXT_V:EXT_V + GMLP_WIDTH])
    lo = lane < 64
    vn_tiles = []
    for p in range(GMLP_GROUPS // 2):
        vp = v[:, p * LANES:(p + 1) * LANES]
        sq = vp * vp
        r_lo = lax.rsqrt(jnp.sum(jnp.where(lo, sq, 0.0), axis=-1, keepdims=True)
                         * (1.0 / GMLP_GROUP_DIM) + EPS)
        r_hi = lax.rsqrt(jnp.sum(jnp.where(lo, 0.0, sq), axis=-1, keepdims=True)
                         * (1.0 / GMLP_GROUP_DIM) + EPS)
        vn = vp * jnp.where(lo, r_lo, r_hi) * vnw_ref[:, p * LANES:(p + 1) * LANES]
        vn_tiles.append(vn.astype(jnp.bfloat16))
    lo_c = _lane_iota((CHUNK, LANES)) < 64
    for c in range(tm // CHUNK):
        rows = slice(c * CHUNK, (c + 1) * CHUNK)
        for p in range(GMLP_GROUPS // 2):
            o = jnp.dot(ws_ref[p], vn_tiles[p][rows], preferred_element_type=jnp.float32)
            s = jnp.where(lo_c, o[:CHUNK], o[CHUNK:]) + bs_ref[:, p * LANES:(p + 1) * LANES]
            sg_ref[0, rows, p * LANES:(p + 1) * LANES] = (
                u[rows, p * LANES:(p + 1) * LANES] * s).astype(jnp.bfloat16)


def _kvonly_kernel(x_ref, mod_ref, nw_ref, win_ref, kvn_ref, wukv_ref, wkn_ref, wka_ref, wkb_ref,
                   tkc_ref, tks_ref, k_ref, v_ref):
    proj = _modulated_proj(x_ref[0], mod_ref, nw_ref, win_ref)
    _kv_prep(proj, kvn_ref, wukv_ref, wkn_ref, wka_ref, wkb_ref, tkc_ref, tks_ref, k_ref, v_ref)


def _prep_call(x, mod, per_batch, wts, tabs, tm, with_q):
    bsz, seq, _ = x.shape
    x_spec = pl.BlockSpec((1, tm, D_MODEL), lambda b, i: (b, i, 0))
    tab_spec = lambda w: pl.BlockSpec((tm, w), lambda b, i: (i, 0))
    ncols = EXT_COLS if with_q else EXT_KV_ONLY
    kv_specs = [x_spec, _mod_spec(1, per_batch), _const_spec((1, D_MODEL)),
                _const_spec((D_MODEL, ncols)), _const_spec((1, KV_LORA)),
                _const_spec((KV_LORA, MLA_HEADS * LANES)), _const_spec((1, LANES)),
                _const_spec((1, LANES)), _const_spec((1, LANES)), tab_spec(LANES), tab_spec(LANES)]
    kv_args = [x, mod, wts["norm2"], wts["w_in"] if with_q else wts["w_in"][:, :EXT_KV_ONLY],
               wts["kvn"], wts["w_ukv"], wts["wk_nope"], wts["wk_a"], wts["wk_b"],
               tabs["kc"], tabs["ks"]]
    k_shape = jax.ShapeDtypeStruct((bsz, MLA_HEADS, seq, HEAD_PAD), jnp.bfloat16)
    v_shape = jax.ShapeDtypeStruct((bsz, seq, MLA_HEADS * LANES), jnp.bfloat16)
    sg_shape = jax.ShapeDtypeStruct((bsz, seq, GMLP_WIDTH), jnp.bfloat16)
    k_spec = pl.BlockSpec((1, MLA_HEADS, tm, HEAD_PAD), lambda b, i: (b, 0, i, 0))
    v_spec = pl.BlockSpec((1, tm, MLA_HEADS * LANES), lambda b, i: (b, i, 0))
    sg_spec = pl.BlockSpec((1, tm, GMLP_WIDTH), lambda b, i: (b, i, 0))
    params = pltpu.CompilerParams(dimension_semantics=("arbitrary", "arbitrary"),
                                  vmem_limit_bytes=VMEM_LIMIT)
    if not with_q:
        return pl.pallas_call(
            _kvonly_kernel, grid=(bsz, seq // tm), in_specs=kv_specs,
            out_specs=[k_spec, v_spec], out_shape=[k_shape, v_shape],
            compiler_params=params, name="kv_prep")(*kv_args)
    q_specs = [_const_spec((1, Q_LORA)), _const_spec((Q_LORA, MLA_HEADS * LANES)),
               _const_spec((1, 2 * LANES)), tab_spec(2 * LANES), _const_spec((1, GMLP_WIDTH)),
               _const_spec((GMLP_GROUPS // 2, 2 * CHUNK, CHUNK)), _const_spec((CHUNK, GMLP_WIDTH))]
    q_args = [wts["qan"], wts["w_uq"], wts["wq"], tabs["q"], wts["vnw"], wts["ws"], wts["bs"]]
    return pl.pallas_call(
        _prep_kernel, grid=(bsz, seq // tm), in_specs=kv_specs + q_specs,
        out_specs=[k_spec, v_spec, k_spec, sg_spec],
        out_shape=[k_shape, v_shape, k_shape, sg_shape],
        compiler_params=params, name="mix_prep")(*kv_args, *q_args)


def _attn_kernel(q_ref, kl_ref, kc_ref, vl_ref, vc_ref, o_ref):
    nt = (((1,), (1,)), ((), ()))
    tq = q_ref.shape[2]
    lane = _lane_iota((tq, LANES))
    lo = lane < 64
    ones_lane = (lane == V_HEAD, lane == 0)
    outs = []
    for h in range(MLA_HEADS):
        q = q_ref[0, h]
        par = h % 2
        pv = slice((h - par) * LANES, (h - par + 2) * LANES)
        s1 = lax.dot_general(q, kl_ref[0, h], nt, preferred_element_type=jnp.float32)
        s2 = lax.dot_general(q, kc_ref[0, h], nt, preferred_element_type=jnp.float32)
        m = jnp.maximum(jnp.max(s1, axis=-1, keepdims=True), jnp.max(s2, axis=-1, keepdims=True))
        p1 = jnp.exp2(s1 - m).astype(jnp.bfloat16)
        p2 = jnp.exp2(s2 - m).astype(jnp.bfloat16)
        o = (jnp.dot(p1, vl_ref[0, :, pv], preferred_element_type=jnp.float32)
             + jnp.dot(p2, vc_ref[0, :, pv], preferred_element_type=jnp.float32))
        o = o[:, par * LANES:(par + 1) * LANES]
        l = jnp.sum(jnp.where(ones_lane[par], o, 0.0), axis=-1, keepdims=True)
        outs.append(o / l)
    for p in range(N_PAIRS):
        o_ref[0, :, p * LANES:(p + 1) * LANES] = jnp.where(
            lo, outs[2 * p], outs[2 * p + 1]).astype(jnp.bfloat16)


def _attn_call(q, k_lat, k_ctx, v_lat, v_ctx, tq):
    bsz, _, seq, _ = q.shape
    n_ctx = k_ctx.shape[2]
    return pl.pallas_call(
        _attn_kernel,
        grid=(bsz, seq // tq),
        in_specs=[pl.BlockSpec((1, MLA_HEADS, tq, HEAD_PAD), lambda b, i: (b, 0, i, 0)),
                  pl.BlockSpec((1, MLA_HEADS, seq, HEAD_PAD), lambda b, i: (b, 0, 0, 0)),
                  pl.BlockSpec((1, MLA_HEADS, n_ctx, HEAD_PAD), lambda b, i: (b, 0, 0, 0)),
                  pl.BlockSpec((1, seq, MLA_HEADS * LANES), lambda b, i: (b, 0, 0)),
                  pl.BlockSpec((1, n_ctx, MLA_HEADS * LANES), lambda b, i: (b, 0, 0))],
        out_specs=pl.BlockSpec((1, tq, MLA_HEADS * V_HEAD), lambda b, i: (b, i, 0)),
        out_shape=jax.ShapeDtypeStruct((bsz, seq, MLA_HEADS * V_HEAD), jnp.bfloat16),
        compiler_params=pltpu.CompilerParams(
            dimension_semantics=("arbitrary", "arbitrary"),
            vmem_limit_bytes=VMEM_LIMIT),
        name="attention",
    )(q, k_lat, k_ctx, v_lat, v_ctx)


def _out_ffn_kernel(x_ref, attn_ref, sg_ref, modm_ref, wout_ref, mod_ref, nw_ref,
                    w1_ref, w3_ref, w2_ref, o_ref, h_ref, acc_ref):
    y = (jnp.dot(attn_ref[0], wout_ref[0], preferred_element_type=jnp.float32)
         + jnp.dot(sg_ref[0], wout_ref[1], preferred_element_type=jnp.float32))
    x = x_ref[0] + modm_ref[0, 2] * y
    o_ref[0] = _ffn_core(x, mod_ref, nw_ref, w1_ref, w3_ref, w2_ref, h_ref, acc_ref)


def _out_ffn_call(x, attn, sg, mod, w_out, norm_w, w1, w3, w2, tm):
    bsz, seq, _ = x.shape
    half = MLA_HEADS * V_HEAD
    row = lambda w: pl.BlockSpec((1, tm, w), lambda b, i: (b, i, 0))
    return pl.pallas_call(
        _out_ffn_kernel,
        grid=(bsz, seq // tm),
        in_specs=[row(D_MODEL), row(half), row(GMLP_WIDTH), _mod_spec(1, True),
                  _const_spec((2, half, D_MODEL)), _mod_spec(2, True), _const_spec((1, D_MODEL)),
                  _const_spec((D_MODEL, D_FF)), _const_spec((D_MODEL, D_FF)),
                  _const_spec((N_FF, FF_TILE, D_MODEL))],
        out_specs=row(D_MODEL),
        out_shape=jax.ShapeDtypeStruct(x.shape, jnp.float32),
        scratch_shapes=[pltpu.VMEM((tm, D_MODEL), jnp.bfloat16),
                        pltpu.VMEM((tm, D_MODEL), jnp.float32)],
        compiler_params=pltpu.CompilerParams(dimension_semantics=("arbitrary", "arbitrary"),
                                             vmem_limit_bytes=VMEM_LIMIT),
        name="out_ffn",
    )(x, attn, sg, mod, w_out, mod, norm_w, w1, w3, w2)


def _ffn_weights(w1, w3, w2):
    bf = jnp.bfloat16
    return w1.astype(bf), w3.astype(bf), w2.astype(bf).reshape(N_FF, FF_TILE, D_MODEL)


def _rot_cols(w, start, signed=True):
    half = AXIS_DIM // 2
    parts = []
    for blk in range(QK_ROPE // half):
        src = start + (blk + 1) * half if blk % 2 == 0 else start + (blk - 1) * half
        piece = w[..., src:src + half]
        parts.append(-piece if (signed and blk % 2 == 0) else piece)
    return jnp.concatenate(parts, axis=-1)


def _mix_weights(norm2_w, w_in, q_a_norm_w, w_uq, kv_a_norm_w, w_ukv, q_norm_w, k_norm_w,
                 v_norm_w, w_s, b_s):
    bf = jnp.bfloat16
    z_in = jnp.zeros((D_MODEL, 64), jnp.float32)
    kpe = w_in[:, KV_LORA:KV_COLS]
    kpe_rot = _rot_cols(w_in, KV_LORA)
    w_in_ext = jnp.concatenate([w_in[:, :KV_LORA], z_in, kpe, kpe, z_in, kpe_rot, kpe_rot,
                                w_in[:, Q_START:]], axis=1).astype(bf)
    ukv = w_ukv.reshape(KV_LORA, MLA_HEADS, QK_NOPE + V_HEAD)
    w_ukv_p = jnp.concatenate([ukv[:, :, :QK_NOPE].reshape(KV_LORA, -1),
                               ukv[:, :, QK_NOPE:].reshape(KV_LORA, -1)], axis=1).astype(bf)
    cols = []
    for h in range(MLA_HEADS):
        base = h * QK_HEAD
        nope = w_uq[:, base:base + QK_NOPE]
        pes = w_uq[:, base + QK_NOPE:base + QK_HEAD]
        rots = _rot_cols(w_uq, base + QK_NOPE)
        cols += [nope, pes, rots] if h % 2 == 0 else [pes, rots, nope]
    w_uq_ext = jnp.concatenate(cols, axis=1).astype(bf)
    qn, qp, qr = q_norm_w[:QK_NOPE], q_norm_w[QK_NOPE:], _rot_cols(q_norm_w, QK_NOPE, signed=False)
    wq = jnp.concatenate([qn, qp, qr, qp, qr, qn])[None]
    kn, kp, kr = k_norm_w[:QK_NOPE], k_norm_w[QK_NOPE:], _rot_cols(k_norm_w, QK_NOPE, signed=False)
    z64 = jnp.zeros((64,), jnp.float32)
    return dict(
        norm2=norm2_w[None], w_in=w_in_ext, kvn=kv_a_norm_w[None], w_ukv=w_ukv_p,
        wk_nope=jnp.concatenate([kn, kn])[None],
        wk_a=jnp.concatenate([z64, kp, kp])[None], wk_b=jnp.concatenate([z64, kr, kr])[None],
        qan=q_a_norm_w[None], w_uq=w_uq_ext, wq=wq,
        vnw=v_norm_w.reshape(1, GMLP_WIDTH),
        ws=w_s.astype(bf).reshape(GMLP_GROUPS // 2, 2 * CHUNK, CHUNK),
        bs=jnp.broadcast_to(b_s.T[:, :, None], (CHUNK, GMLP_GROUPS, GMLP_GROUP_DIM)
                            ).reshape(CHUNK, GMLP_WIDTH),
    )


def _rope_tables(seq, n_ctx):
    f32 = jnp.float32
    rows_n = seq // GRID_W
    rows = jnp.repeat(jnp.arange(rows_n, dtype=f32), GRID_W)
    cols = jnp.tile(jnp.arange(GRID_W, dtype=f32), rows_n)
    inv = ROPE_BASE ** (-jnp.arange(0, AXIS_DIM, 2, dtype=f32) / AXIS_DIM)
    ang_r = rows[:, None] * inv
    ang_c = cols[:, None] * inv
    ang = jnp.concatenate([ang_r, ang_r, ang_c, ang_c], axis=-1)
    cos, sin = jnp.cos(ang), jnp.sin(ang)
    one = jnp.ones((seq, 64), f32)
    zero = jnp.zeros((seq, 64), f32)
    lat = dict(kc=jnp.concatenate([zero, cos, cos], 1), ks=jnp.concatenate([zero, sin, sin], 1),
               q=jnp.concatenate([one, cos, sin, cos, sin, one], 1))
    ctx = dict(kc=jnp.concatenate([jnp.zeros((n_ctx, 64), f32), jnp.ones((n_ctx, 64), f32)], 1),
               ks=jnp.zeros((n_ctx, LANES), f32))
    return lat, ctx


def kernel(x, c, ctx, c_ctx, w_ada, b_ada, norm1_w, ffn1_w1, ffn1_w3, ffn1_w2, norm2_w, w_in,
           q_a_norm_w, w_uq, kv_a_norm_w, w_ukv, q_norm_w, k_norm_w, v_norm_w, w_s, b_s, w_out,
           norm3_w, ffn2_w1, ffn2_w3, ffn2_w2):
    bsz, seq, _ = x.shape
    n_ctx = ctx.shape[1]
    rows = -(-(bsz + 1) // 8) * 8
    cc = jnp.concatenate([c, c_ctx[None], jnp.zeros((rows - bsz - 1, D_MODEL), jnp.float32)], 0)
    mod = _ada_call(cc, w_ada[0], b_ada[0][None]).reshape(rows, N_MOD, 1, D_MODEL)
    mod_ctx = mod[bsz:bsz + 1]

    f1 = _ffn_weights(ffn1_w1[0], ffn1_w3[0], ffn1_w2[0])
    f2 = _ffn_weights(ffn2_w1[0], ffn2_w3[0], ffn2_w2[0])
    wts = _mix_weights(norm2_w[0], w_in[0], q_a_norm_w[0], w_uq[0], kv_a_norm_w[0], w_ukv[0],
                       q_norm_w[0], k_norm_w[0], v_norm_w[0], w_s[0], b_s[0])
    tabs_lat, tabs_ctx = _rope_tables(seq, n_ctx)

    x1 = _ffn_call(x, mod, 0, True, norm1_w, *f1, tm=512)
    ctx1 = _ffn_call(ctx, mod_ctx, 0, False, norm1_w, *f1, tm=n_ctx)
    k_lat, v_lat, q, sg = _prep_call(x1, mod, True, wts, tabs_lat, tm=512, with_q=True)
    k_ctx, v_ctx = _prep_call(ctx1, mod_ctx, False, wts, tabs_ctx, tm=n_ctx, with_q=False)
    attn = _attn_call(q, k_lat, k_ctx, v_lat, v_ctx, tq=256)
    w_out_r = w_out[0].astype(jnp.bfloat16).reshape(2, MLA_HEADS * V_HEAD, D_MODEL)
    return _out_ffn_call(x1, attn, sg, mod, w_out_r, norm3_w, *f2, tm=512)
```

```python
import numpy as np
import jax
import jax.numpy as jnp
from jax import lax
from jax.experimental import pallas as pl
from jax.experimental.pallas import tpu as pltpu

D_MODEL = 1024
GRID_W = 64
MLA_HEADS = 8
QK_NOPE = 64
QK_ROPE = 32
QK_HEAD = QK_NOPE + QK_ROPE
V_HEAD = 64
Q_LORA = 256
KV_LORA = 128
AXIS_DIM = QK_ROPE // 2
ROPE_BASE = 10000.0
GMLP_GROUPS = 8
GMLP_GROUP_DIM = 64
GMLP_WIDTH = GMLP_GROUPS * GMLP_GROUP_DIM
CHUNK = 128
KV_COLS = KV_LORA + QK_ROPE
Q_START = KV_COLS
U_START = KV_COLS + Q_LORA
V_START = U_START + GMLP_WIDTH
IN_COLS = V_START + GMLP_WIDTH
D_FF = 2816
N_MOD = 9
EPS = 1e-6

LANES = 128
HEAD_PAD = LANES
N_PAIRS = MLA_HEADS // 2
VMEM_LIMIT = 56 * 1024 * 1024

EXT_KV = 0
EXT_AB = 128
EXT_Q = 256
EXT_U = EXT_Q + Q_LORA
EXT_V = EXT_U + GMLP_WIDTH
EXT_COLS = EXT_V + GMLP_WIDTH
EXT_KV_ONLY = EXT_Q

FF_TILE = 256
N_FF = D_FF // FF_TILE
ROW_TILE = 512
Q_TILE = 512


def _rms_scale(x, n):
    return lax.rsqrt(jnp.sum(x * x, axis=-1, keepdims=True) * (1.0 / n) + EPS)


def _silu(a):
    return a / (1.0 + jnp.exp(-a))


def _gelu_tanh(x):
    c = np.float32(np.sqrt(2.0 / np.pi))
    t = jnp.tanh(x * (c + np.float32(c * 0.044715) * (x * x)))
    return x * (0.5 + 0.5 * t)


def _ada_kernel(c_ref, w_ref, b_ref, o_ref):
    s = _silu(c_ref[...]).astype(jnp.bfloat16)
    o_ref[...] = jnp.dot(s, w_ref[...].astype(jnp.bfloat16),
                         preferred_element_type=jnp.float32) + b_ref[...]


def _ada_call(cc, w_ada, b_ada):
    rows = cc.shape[0]
    n = w_ada.shape[1]
    tn = 1024
    return pl.pallas_call(
        _ada_kernel,
        grid=(n // tn,),
        in_specs=[pl.BlockSpec((rows, D_MODEL), lambda j: (0, 0)),
                  pl.BlockSpec((D_MODEL, tn), lambda j: (0, j)),
                  pl.BlockSpec((1, tn), lambda j: (0, j))],
        out_specs=pl.BlockSpec((rows, tn), lambda j: (0, j)),
        out_shape=jax.ShapeDtypeStruct((rows, n), jnp.float32),
        compiler_params=pltpu.CompilerParams(dimension_semantics=("arbitrary",),
                                             vmem_limit_bytes=VMEM_LIMIT),
        name="adaln",
    )(cc, w_ada, b_ada)


def _ffn_core(x, mod_ref, nw_ref, w1_ref, w3_ref, w2_ref, h_ref, acc_ref):
    shift, scale, gate = mod_ref[0, 0], mod_ref[0, 1], mod_ref[0, 2]
    h_ref[...] = (x * _rms_scale(x, D_MODEL) * (nw_ref[...] * (1.0 + scale)) + shift
                  ).astype(jnp.bfloat16)
    acc_ref[...] = jnp.zeros_like(acc_ref)

    for j in range(N_FF):
        hb = h_ref[...]
        cols = slice(j * FF_TILE, (j + 1) * FF_TILE)
        a = jnp.dot(hb, w1_ref[:, cols], preferred_element_type=jnp.float32)
        b = jnp.dot(hb, w3_ref[:, cols], preferred_element_type=jnp.float32)
        g = (_silu(a) * b).astype(jnp.bfloat16)
        acc_ref[...] += jnp.dot(g, w2_ref[j], preferred_element_type=jnp.float32)
    return x + (0.5 * gate) * acc_ref[...]


def _ffn_kernel(x_ref, mod_ref, nw_ref, w1_ref, w3_ref, w2_ref, o_ref, h_ref, acc_ref):
    o_ref[0] = _ffn_core(x_ref[0], mod_ref, nw_ref, w1_ref, w3_ref, w2_ref, h_ref, acc_ref)


def _const_spec(shape):
    nd = len(shape)
    return pl.BlockSpec(shape, lambda *_: (0,) * nd, pipeline_mode=pl.Buffered(1))


def _mod_spec(mod_block, per_batch):
    if per_batch:
        return pl.BlockSpec((1, 3, 1, D_MODEL), lambda b, i: (b, mod_block, 0, 0))
    return pl.BlockSpec((1, 3, 1, D_MODEL), lambda b, i: (0, mod_block, 0, 0))


def _ffn_call(x, mod, mod_block, per_batch, norm_w, w1, w3, w2, tm):
    bsz, seq, _ = x.shape
    return pl.pallas_call(
        _ffn_kernel,
        grid=(bsz, seq // tm),
        in_specs=[pl.BlockSpec((1, tm, D_MODEL), lambda b, i: (b, i, 0)),
                  _mod_spec(mod_block, per_batch),
                  _const_spec((1, D_MODEL)),
                  _const_spec((D_MODEL, D_FF)),
                  _const_spec((D_MODEL, D_FF)),
                  _const_spec((N_FF, FF_TILE, D_MODEL))],
        out_specs=pl.BlockSpec((1, tm, D_MODEL), lambda b, i: (b, i, 0)),
        out_shape=jax.ShapeDtypeStruct(x.shape, jnp.float32),
        scratch_shapes=[pltpu.VMEM((tm, D_MODEL), jnp.bfloat16),
                        pltpu.VMEM((tm, D_MODEL), jnp.float32)],
        compiler_params=pltpu.CompilerParams(dimension_semantics=("arbitrary", "arbitrary"),
                                             vmem_limit_bytes=VMEM_LIMIT),
        name="ffn",
    )(x, mod, norm_w, w1, w3, w2)


def _lane_iota(shape):
    return lax.broadcasted_iota(jnp.int32, shape, len(shape) - 1)


def _sq_bf16(x):
    return (x * x).astype(jnp.bfloat16)


def _seg_rms(sq, sel_ref, n):
    ss = jnp.dot(sq, sel_ref[...], preferred_element_type=jnp.float32)
    return lax.rsqrt(ss * (1.0 / n) + EPS)


def _kv_prep(proj, kvn_ref, wukv_ref, wkn_ref, wkab_ref, tk_ref, selk_ref, k_ref, v_ref):
    tm = proj.shape[0]
    ckv = proj[:, EXT_KV:EXT_KV + KV_LORA]
    ckv = (ckv * _rms_scale(ckv, KV_LORA) * kvn_ref[...]).astype(jnp.bfloat16)
    kv = jnp.dot(ckv, wukv_ref[...], preferred_element_type=jnp.float32)
    lane = _lane_iota((tm, LANES))
    lo = lane < 64
    ab = proj[:, EXT_AB:EXT_AB + LANES]
    t = ab * (wkab_ref[...] * tk_ref[...])
    rope = t + pltpu.roll(t, 64, axis=1)
    ab_sq = _sq_bf16(ab)
    one_e = jnp.where(lane == V_HEAD, 1.0, 0.0)
    one_o = jnp.where(lane == 0, 1.0, 0.0)
    for p in range(N_PAIRS):
        vp = kv[:, MLA_HEADS * QK_NOPE + p * LANES:MLA_HEADS * QK_NOPE + (p + 1) * LANES]
        v_ref[0, :, (2 * p) * LANES:(2 * p + 1) * LANES] = jnp.where(lo, vp, one_e).astype(jnp.bfloat16)
        v_ref[0, :, (2 * p + 1) * LANES:(2 * p + 2) * LANES] = jnp.where(lo, one_o, vp).astype(jnp.bfloat16)
        kp = kv[:, p * LANES:(p + 1) * LANES]
        r = _seg_rms(jnp.concatenate([_sq_bf16(kp), ab_sq], axis=1), selk_ref, QK_HEAD)
        kw = kp * wkn_ref[...]
        k_ref[0, 2 * p] = (jnp.where(lo, kw, rope) * r[:, :LANES]).astype(jnp.bfloat16)
        k_ref[0, 2 * p + 1] = (jnp.where(lo, rope, kw) * r[:, LANES:]).astype(jnp.bfloat16)


def _modulated_proj(x, mod_ref, nw_ref, win_ref):
    shift, scale = mod_ref[0, 0], mod_ref[0, 1]
    h = (x * _rms_scale(x, D_MODEL) * (nw_ref[...] * (1.0 + scale)) + shift).astype(jnp.bfloat16)
    return jnp.dot(h, win_ref[...], preferred_element_type=jnp.float32)


def _prep_kernel(x_ref, mod_ref, nw_ref, win_ref, kvn_ref, wukv_ref, wkn_ref, wkab_ref, tk_ref,
                 selk_ref, qan_ref, wuq_ref, wqn_ref, tq_ref, selq_ref, vnw_ref, selv_ref,
                 ws_ref, bs_ref, k_ref, v_ref, q_ref, sg_ref):
    tm = x_ref.shape[1]
    proj = _modulated_proj(x_ref[0], mod_ref, nw_ref, win_ref)
    _kv_prep(proj, kvn_ref, wukv_ref, wkn_ref, wkab_ref, tk_ref, selk_ref, k_ref, v_ref)

    cq = proj[:, EXT_Q:EXT_Q + Q_LORA]
    cq = (cq * _rms_scale(cq, Q_LORA) * qan_ref[...]).astype(jnp.bfloat16)
    qall = jnp.dot(cq, wuq_ref[...], preferred_element_type=jnp.float32)
    lane = _lane_iota((tm, LANES))
    tabs = [wqn_ref[:, par * LANES:(par + 1) * LANES] * tq_ref[:, par * LANES:(par + 1) * LANES]
            * np.float32(QK_HEAD ** -0.5 * np.log2(np.e)) for par in range(2)]
    for p in range(N_PAIRS):
        qp = qall[:, 2 * p * LANES:(2 * p + 2) * LANES]
        r = _seg_rms(_sq_bf16(qp), selq_ref, QK_HEAD)
        for par in range(2):
            hl = slice(par * LANES, (par + 1) * LANES)
            q_ref[0, 2 * p + par] = (qp[:, hl] * r[:, hl] * tabs[par]).astype(jnp.bfloat16)

    u = _gelu_tanh(proj[:, EXT_U:EXT_U + GMLP_WIDTH])
    v = _gelu_tanh(proj[:, EXT_V:EXT_V + GMLP_WIDTH])
    lo = lane < 64
    vn_tiles = []
    for p in range(GMLP_GROUPS // 2):
        if p % 2 == 0:
            v2 = v[:, p * LANES:(p + 2) * LANES]
            vn2 = (v2 * _seg_rms(_sq_bf16(v2), selv_ref, GMLP_GROUP_DIM)
                   * vnw_ref[:, p * LANES:(p + 2) * LANES])
        vn = vn2[:, (p % 2) * LANES:(p % 2 + 1) * LANES]
        vn_tiles.append((jnp.where(lo, vn, 0.0).astype(jnp.bfloat16),
                         jnp.where(lo, 0.0, vn).astype(jnp.bfloat16)))
    for c in range(tm // CHUNK):
        rows = slice(c * CHUNK, (c + 1) * CHUNK)
        for p in range(GMLP_GROUPS // 2):
            vblk = jnp.concatenate([vn_tiles[p][0][rows], vn_tiles[p][1][rows]], axis=0)
            s = (jnp.dot(ws_ref[p], vblk, preferred_element_type=jnp.float32)
                 + bs_ref[:, p * LANES:(p + 1) * LANES])
            sg_ref[0, rows, p * LANES:(p + 1) * LANES] = (
                u[rows, p * LANES:(p + 1) * LANES] * s).astype(jnp.bfloat16)


def _kvonly_kernel(x_ref, mod_ref, nw_ref, win_ref, kvn_ref, wukv_ref, wkn_ref, wkab_ref, tk_ref,
                   selk_ref, k_ref, v_ref):
    proj = _modulated_proj(x_ref[0], mod_ref, nw_ref, win_ref)
    _kv_prep(proj, kvn_ref, wukv_ref, wkn_ref, wkab_ref, tk_ref, selk_ref, k_ref, v_ref)


def _prep_call(x, mod, per_batch, wts, tabs, tm, with_q):
    bsz, seq, _ = x.shape
    x_spec = pl.BlockSpec((1, tm, D_MODEL), lambda b, i: (b, i, 0))
    tab_spec = lambda w: pl.BlockSpec((tm, w), lambda b, i: (i, 0))
    ncols = EXT_COLS if with_q else EXT_KV_ONLY
    kv_specs = [x_spec, _mod_spec(1, per_batch), _const_spec((1, D_MODEL)),
                _const_spec((D_MODEL, ncols)), _const_spec((1, KV_LORA)),
                _const_spec((KV_LORA, MLA_HEADS * LANES)), _const_spec((1, LANES)),
                _const_spec((1, LANES)), tab_spec(LANES), _const_spec((2 * LANES, 2 * LANES))]
    kv_args = [x, mod, wts["norm2"], wts["w_in"] if with_q else wts["w_in"][:, :EXT_KV_ONLY],
               wts["kvn"], wts["w_ukv"], wts["wk_nope"], wts["wk_ab"], tabs["k"], wts["sel_k"]]
    k_shape = jax.ShapeDtypeStruct((bsz, MLA_HEADS, seq, HEAD_PAD), jnp.bfloat16)
    v_shape = jax.ShapeDtypeStruct((bsz, seq, MLA_HEADS * LANES), jnp.bfloat16)
    sg_shape = jax.ShapeDtypeStruct((bsz, seq, GMLP_WIDTH), jnp.bfloat16)
    k_spec = pl.BlockSpec((1, MLA_HEADS, tm, HEAD_PAD), lambda b, i: (b, 0, i, 0))
    v_spec = pl.BlockSpec((1, tm, MLA_HEADS * LANES), lambda b, i: (b, i, 0))
    sg_spec = pl.BlockSpec((1, tm, GMLP_WIDTH), lambda b, i: (b, i, 0))
    params = pltpu.CompilerParams(dimension_semantics=("arbitrary", "arbitrary"),
                                  vmem_limit_bytes=VMEM_LIMIT)
    if not with_q:
        return pl.pallas_call(
            _kvonly_kernel, grid=(bsz, seq // tm), in_specs=kv_specs,
            out_specs=[k_spec, v_spec], out_shape=[k_shape, v_shape],
            compiler_params=params, name="kv_prep")(*kv_args)
    q_specs = [_const_spec((1, Q_LORA)), _const_spec((Q_LORA, MLA_HEADS * LANES)),
               _const_spec((1, 2 * LANES)), tab_spec(2 * LANES), _const_spec((2 * LANES, 2 * LANES)),
               _const_spec((1, GMLP_WIDTH)), _const_spec((2 * LANES, 2 * LANES)),
               _const_spec((GMLP_GROUPS // 2, CHUNK, 2 * CHUNK)), _const_spec((CHUNK, GMLP_WIDTH))]
    q_args = [wts["qan"], wts["w_uq"], wts["wq"], tabs["q"], wts["sel_q"], wts["vnw"], wts["sel_v"],
              wts["ws"], wts["bs"]]
    return pl.pallas_call(
        _prep_kernel, grid=(bsz, seq // tm), in_specs=kv_specs + q_specs,
        out_specs=[k_spec, v_spec, k_spec, sg_spec],
        out_shape=[k_shape, v_shape, k_shape, sg_shape],
        compiler_params=params, name="mix_prep")(*kv_args, *q_args)


def _attn_kernel(q_ref, kl_ref, kc_ref, vl_ref, vc_ref, o_ref):
    nt = (((1,), (1,)), ((), ()))
    tq = q_ref.shape[2]
    lane = _lane_iota((tq, LANES))
    lo = lane < 64
    ones_lane = (lane == V_HEAD, lane == 0)
    outs = []
    for h in range(MLA_HEADS):
        q = q_ref[0, h]
        par = h % 2
        pv = slice((h - par) * LANES, (h - par + 2) * LANES)
        s1 = lax.dot_general(q, kl_ref[0, h], nt, preferred_element_type=jnp.float32)
        s2 = lax.dot_general(q, kc_ref[0, h], nt, preferred_element_type=jnp.float32)
        m = jnp.maximum(jnp.max(s1, axis=-1, keepdims=True), jnp.max(s2, axis=-1, keepdims=True))
        p1 = jnp.exp2(s1 - m).astype(jnp.bfloat16)
        p2 = jnp.exp2(s2 - m).astype(jnp.bfloat16)
        o = (jnp.dot(p1, vl_ref[0, :, pv], preferred_element_type=jnp.float32)
             + jnp.dot(p2, vc_ref[0, :, pv], preferred_element_type=jnp.float32))
        o = o[:, par * LANES:(par + 1) * LANES]
        l = jnp.sum(jnp.where(ones_lane[par], o, 0.0), axis=-1, keepdims=True)
        outs.append(o / l)
    for p in range(N_PAIRS):
        o_ref[0, :, p * LANES:(p + 1) * LANES] = jnp.where(
            lo, outs[2 * p], outs[2 * p + 1]).astype(jnp.bfloat16)


def _attn_call(q, k_lat, k_ctx, v_lat, v_ctx, tq):
    bsz, _, seq, _ = q.shape
    n_ctx = k_ctx.shape[2]
    return pl.pallas_call(
        _attn_kernel,
        grid=(bsz, seq // tq),
        in_specs=[pl.BlockSpec((1, MLA_HEADS, tq, HEAD_PAD), lambda b, i: (b, 0, i, 0)),
                  pl.BlockSpec((1, MLA_HEADS, seq, HEAD_PAD), lambda b, i: (b, 0, 0, 0)),
                  pl.BlockSpec((1, MLA_HEADS, n_ctx, HEAD_PAD), lambda b, i: (b, 0, 0, 0)),
                  pl.BlockSpec((1, seq, MLA_HEADS * LANES), lambda b, i: (b, 0, 0)),
                  pl.BlockSpec((1, n_ctx, MLA_HEADS * LANES), lambda b, i: (b, 0, 0))],
        out_specs=pl.BlockSpec((1, tq, MLA_HEADS * V_HEAD), lambda b, i: (b, i, 0)),
        out_shape=jax.ShapeDtypeStruct((bsz, seq, MLA_HEADS * V_HEAD), jnp.bfloat16),
        compiler_params=pltpu.CompilerParams(
            dimension_semantics=("arbitrary", "arbitrary"),
            vmem_limit_bytes=VMEM_LIMIT),
        name="attention",
    )(q, k_lat, k_ctx, v_lat, v_ctx)


def _out_ffn_kernel(x_ref, attn_ref, sg_ref, modm_ref, wout_ref, mod_ref, nw_ref,
                    w1_ref, w3_ref, w2_ref, o_ref, h_ref, acc_ref):
    y = (jnp.dot(attn_ref[0], wout_ref[0], preferred_element_type=jnp.float32)
         + jnp.dot(sg_ref[0], wout_ref[1], preferred_element_type=jnp.float32))
    x = x_ref[0] + modm_ref[0, 2] * y
    o_ref[0] = _ffn_core(x, mod_ref, nw_ref, w1_ref, w3_ref, w2_ref, h_ref, acc_ref)


def _out_ffn_call(x, attn, sg, mod, w_out, norm_w, w1, w3, w2, tm):
    bsz, seq, _ = x.shape
    half = MLA_HEADS * V_HEAD
    row = lambda w: pl.BlockSpec((1, tm, w), lambda b, i: (b, i, 0))
    return pl.pallas_call(
        _out_ffn_kernel,
        grid=(bsz, seq // tm),
        in_specs=[row(D_MODEL), row(half), row(GMLP_WIDTH), _mod_spec(1, True),
                  _const_spec((2, half, D_MODEL)), _mod_spec(2, True), _const_spec((1, D_MODEL)),
                  _const_spec((D_MODEL, D_FF)), _const_spec((D_MODEL, D_FF)),
                  _const_spec((N_FF, FF_TILE, D_MODEL))],
        out_specs=row(D_MODEL),
        out_shape=jax.ShapeDtypeStruct(x.shape, jnp.float32),
        scratch_shapes=[pltpu.VMEM((tm, D_MODEL), jnp.bfloat16),
                        pltpu.VMEM((tm, D_MODEL), jnp.float32)],
        compiler_params=pltpu.CompilerParams(dimension_semantics=("arbitrary", "arbitrary"),
                                             vmem_limit_bytes=VMEM_LIMIT),
        name="out_ffn",
    )(x, attn, sg, mod, w_out, mod, norm_w, w1, w3, w2)


def _ffn_weights(w1, w3, w2):
    bf = jnp.bfloat16
    return w1.astype(bf), w3.astype(bf), w2.astype(bf).reshape(N_FF, FF_TILE, D_MODEL)


def _rot_cols(w, start, signed=True):
    half = AXIS_DIM // 2
    parts = []
    for blk in range(QK_ROPE // half):
        src = start + (blk + 1) * half if blk % 2 == 0 else start + (blk - 1) * half
        piece = w[..., src:src + half]
        parts.append(-piece if (signed and blk % 2 == 0) else piece)
    return jnp.concatenate(parts, axis=-1)


def _selectors():
    r = np.arange(2 * LANES)[:, None]
    c = np.arange(2 * LANES)[None, :]
    sel_k = ((r < 64) & (c < LANES)) | ((r >= 64) & (r < LANES) & (c >= LANES)) | (
        (r >= LANES) & (r < LANES + QK_ROPE))
    ro = r - LANES
    sel_q = ((r < QK_HEAD) & (c < LANES)) | (
        (r >= LANES) & ((ro < QK_ROPE) | (ro >= 2 * QK_ROPE)) & (c >= LANES))
    sel_v = (r // GMLP_GROUP_DIM) == (c // GMLP_GROUP_DIM)
    return [jnp.asarray(m, jnp.bfloat16) for m in (sel_k, sel_q, sel_v)]


def _mix_weights(norm2_w, w_in, q_a_norm_w, w_uq, kv_a_norm_w, w_ukv, q_norm_w, k_norm_w,
                 v_norm_w, w_s, b_s):
    bf = jnp.bfloat16
    kpe = w_in[:, KV_LORA:KV_COLS]
    kpe_rot = _rot_cols(w_in, KV_LORA)
    w_in_ext = jnp.concatenate([w_in[:, :KV_LORA], kpe, kpe, kpe_rot, kpe_rot,
                                w_in[:, Q_START:]], axis=1).astype(bf)
    ukv = w_ukv.reshape(KV_LORA, MLA_HEADS, QK_NOPE + V_HEAD)
    w_ukv_p = jnp.concatenate([ukv[:, :, :QK_NOPE].reshape(KV_LORA, -1),
                               ukv[:, :, QK_NOPE:].reshape(KV_LORA, -1)], axis=1).astype(bf)
    cols = []
    for h in range(MLA_HEADS):
        base = h * QK_HEAD
        nope = w_uq[:, base:base + QK_NOPE]
        pes = w_uq[:, base + QK_NOPE:base + QK_HEAD]
        rots = _rot_cols(w_uq, base + QK_NOPE)
        cols += [nope, pes, rots] if h % 2 == 0 else [pes, rots, nope]
    w_uq_ext = jnp.concatenate(cols, axis=1).astype(bf)
    qn, qp, qr = q_norm_w[:QK_NOPE], q_norm_w[QK_NOPE:], _rot_cols(q_norm_w, QK_NOPE, signed=False)
    wq = jnp.concatenate([qn, qp, qr, qp, qr, qn])[None]
    kn, kp, kr = k_norm_w[:QK_NOPE], k_norm_w[QK_NOPE:], _rot_cols(k_norm_w, QK_NOPE, signed=False)
    sel_k, sel_q, sel_v = _selectors()
    return dict(
        sel_k=sel_k, sel_q=sel_q, sel_v=sel_v,
        norm2=norm2_w[None], w_in=w_in_ext, kvn=kv_a_norm_w[None], w_ukv=w_ukv_p,
        wk_nope=jnp.concatenate([kn, kn])[None],
        wk_ab=jnp.concatenate([kp, kp, kr, kr])[None],
        qan=q_a_norm_w[None], w_uq=w_uq_ext, wq=wq,
        vnw=v_norm_w.reshape(1, GMLP_WIDTH),
        ws=w_s.astype(bf).reshape(GMLP_GROUPS // 2, 2, CHUNK, CHUNK).transpose(0, 2, 1, 3
                                   ).reshape(GMLP_GROUPS // 2, CHUNK, 2 * CHUNK),
        bs=jnp.broadcast_to(b_s.T[:, :, None], (CHUNK, GMLP_GROUPS, GMLP_GROUP_DIM)
                            ).reshape(CHUNK, GMLP_WIDTH),
    )


def _rope_tables(seq, n_ctx):
    f32 = jnp.float32
    rows_n = seq // GRID_W
    rows = jnp.repeat(jnp.arange(rows_n, dtype=f32), GRID_W)
    cols = jnp.tile(jnp.arange(GRID_W, dtype=f32), rows_n)
    inv = ROPE_BASE ** (-jnp.arange(0, AXIS_DIM, 2, dtype=f32) / AXIS_DIM)
    ang_r = rows[:, None] * inv
    ang_c = cols[:, None] * inv
    ang = jnp.concatenate([ang_r, ang_r, ang_c, ang_c], axis=-1)
    cos, sin = jnp.cos(ang), jnp.sin(ang)
    one = jnp.ones((seq, 64), f32)
    lat = dict(k=jnp.concatenate([cos, cos, sin, sin], 1),
               q=jnp.concatenate([one, cos, sin, cos, sin, one], 1))
    ctx = dict(k=jnp.concatenate([jnp.ones((n_ctx, 64), f32), jnp.zeros((n_ctx, 64), f32)], 1))
    return lat, ctx


def kernel(x, c, ctx, c_ctx, w_ada, b_ada, norm1_w, ffn1_w1, ffn1_w3, ffn1_w2, norm2_w, w_in,
           q_a_norm_w, w_uq, kv_a_norm_w, w_ukv, q_norm_w, k_norm_w, v_norm_w, w_s, b_s, w_out,
           norm3_w, ffn2_w1, ffn2_w3, ffn2_w2):
    bsz, seq, _ = x.shape
    n_ctx = ctx.shape[1]
    rows = -(-(bsz + 1) // 8) * 8
    cc = jnp.concatenate([c, c_ctx[None], jnp.zeros((rows - bsz - 1, D_MODEL), jnp.float32)], 0)
    mod = _ada_call(cc, w_ada[0], b_ada[0][None]).reshape(rows, N_MOD, 1, D_MODEL)
    mod_ctx = mod[bsz:bsz + 1]

    f1 = _ffn_weights(ffn1_w1[0], ffn1_w3[0], ffn1_w2[0])
    f2 = _ffn_weights(ffn2_w1[0], ffn2_w3[0], ffn2_w2[0])
    wts = _mix_weights(norm2_w[0], w_in[0], q_a_norm_w[0], w_uq[0], kv_a_norm_w[0], w_ukv[0],
                       q_norm_w[0], k_norm_w[0], v_norm_w[0], w_s[0], b_s[0])
    tabs_lat, tabs_ctx = _rope_tables(seq, n_ctx)

    x1 = _ffn_call(x, mod, 0, True, norm1_w, *f1, tm=ROW_TILE)
    ctx1 = _ffn_call(ctx.reshape(1, bsz * n_ctx, D_MODEL), mod_ctx, 0, False, norm1_w, *f1,
                     tm=ROW_TILE).reshape(bsz, n_ctx, D_MODEL)
    k_lat, v_lat, q, sg = _prep_call(x1, mod, True, wts, tabs_lat, tm=ROW_TILE, with_q=True)
    k_ctx, v_ctx = _prep_call(ctx1, mod_ctx, False, wts, tabs_ctx, tm=n_ctx, with_q=False)
    attn = _attn_call(q, k_lat, k_ctx, v_lat, v_ctx, tq=Q_TILE)
    w_out_r = w_out[0].astype(jnp.bfloat16).reshape(2, MLA_HEADS * V_HEAD, D_MODEL)
    return _out_ffn_call(x1, attn, sg, mod, w_out_r, norm3_w, *f2, tm=ROW_TILE)
```

```python
import numpy as np
import jax
import jax.numpy as jnp
from jax import lax
from jax.experimental import pallas as pl
from jax.experimental.pallas import tpu as pltpu

D_MODEL = 1024
GRID_W = 64
MLA_HEADS = 8
QK_NOPE = 64
QK_ROPE = 32
QK_HEAD = QK_NOPE + QK_ROPE
V_HEAD = 64
Q_LORA = 256
KV_LORA = 128
AXIS_DIM = QK_ROPE // 2
ROPE_BASE = 10000.0
GMLP_GROUPS = 8
GMLP_GROUP_DIM = 64
GMLP_WIDTH = GMLP_GROUPS * GMLP_GROUP_DIM
CHUNK = 128
KV_COLS = KV_LORA + QK_ROPE
Q_START = KV_COLS
U_START = KV_COLS + Q_LORA
V_START = U_START + GMLP_WIDTH
IN_COLS = V_START + GMLP_WIDTH
D_FF = 2816
N_MOD = 9
EPS = 1e-6

LANES = 128
HEAD_PAD = LANES
N_PAIRS = MLA_HEADS // 2
VMEM_LIMIT = 56 * 1024 * 1024

EXT_KV = 0
EXT_AB = 128
EXT_Q = 256
EXT_U = EXT_Q + Q_LORA
EXT_V = EXT_U + GMLP_WIDTH
EXT_COLS = EXT_V + GMLP_WIDTH
EXT_KV_ONLY = EXT_Q

FF_TILE = 256
N_FF = D_FF // FF_TILE
ROW_TILE = 512
Q_TILE = 1024
Q_SUB = 512


def _rms_scale(x, n):
    return lax.rsqrt(jnp.sum(x * x, axis=-1, keepdims=True) * (1.0 / n) + EPS)


def _silu(a):
    return a / (1.0 + jnp.exp(-a))


def _gelu_tanh(x):
    c = np.float32(np.sqrt(2.0 / np.pi))
    t = jnp.tanh(x * (c + np.float32(c * 0.044715) * (x * x)))
    return x * (0.5 + 0.5 * t)


def _ada_kernel(c_ref, w_ref, b_ref, o_ref):
    s = _silu(c_ref[...]).astype(jnp.bfloat16)
    o_ref[...] = jnp.dot(s, w_ref[...].astype(jnp.bfloat16),
                         preferred_element_type=jnp.float32) + b_ref[...]


def _ada_call(cc, w_ada, b_ada):
    rows = cc.shape[0]
    n = w_ada.shape[1]
    tn = 1024
    return pl.pallas_call(
        _ada_kernel,
        grid=(n // tn,),
        in_specs=[pl.BlockSpec((rows, D_MODEL), lambda j: (0, 0)),
                  pl.BlockSpec((D_MODEL, tn), lambda j: (0, j)),
                  pl.BlockSpec((1, tn), lambda j: (0, j))],
        out_specs=pl.BlockSpec((rows, tn), lambda j: (0, j)),
        out_shape=jax.ShapeDtypeStruct((rows, n), jnp.float32),
        compiler_params=pltpu.CompilerParams(dimension_semantics=("arbitrary",),
                                             vmem_limit_bytes=VMEM_LIMIT),
        name="adaln",
    )(cc, w_ada, b_ada)


def _ffn_core(x, mod_ref, nw_ref, w1_ref, w3_ref, w2_ref, h_ref, acc_ref):
    shift, scale, gate = mod_ref[0, 0], mod_ref[0, 1], mod_ref[0, 2]
    h_ref[...] = (x * _rms_scale(x, D_MODEL) * (nw_ref[...] * (1.0 + scale)) + shift
                  ).astype(jnp.bfloat16)
    acc_ref[...] = jnp.zeros_like(acc_ref)

    for j in range(N_FF):
        hb = h_ref[...]
        cols = slice(j * FF_TILE, (j + 1) * FF_TILE)
        a = jnp.dot(hb, w1_ref[:, cols], preferred_element_type=jnp.float32)
        b = jnp.dot(hb, w3_ref[:, cols], preferred_element_type=jnp.float32)
        g = (_silu(a) * b).astype(jnp.bfloat16)
        acc_ref[...] += jnp.dot(g, w2_ref[j], preferred_element_type=jnp.float32)
    return x + (0.5 * gate) * acc_ref[...]


def _ffn_kernel(x_ref, mod_ref, nw_ref, w1_ref, w3_ref, w2_ref, o_ref, h_ref, acc_ref):
    o_ref[0] = _ffn_core(x_ref[0], mod_ref, nw_ref, w1_ref, w3_ref, w2_ref, h_ref, acc_ref)


def _const_spec(shape):
    nd = len(shape)
    return pl.BlockSpec(shape, lambda *_: (0,) * nd, pipeline_mode=pl.Buffered(1))


def _mod_spec(mod_block, per_batch):
    if per_batch:
        return pl.BlockSpec((1, 3, 1, D_MODEL), lambda b, i: (b, mod_block, 0, 0))
    return pl.BlockSpec((1, 3, 1, D_MODEL), lambda b, i: (0, mod_block, 0, 0))


def _ffn_call(x, mod, mod_block, per_batch, norm_w, w1, w3, w2, tm):
    bsz, seq, _ = x.shape
    return pl.pallas_call(
        _ffn_kernel,
        grid=(bsz, seq // tm),
        in_specs=[pl.BlockSpec((1, tm, D_MODEL), lambda b, i: (b, i, 0)),
                  _mod_spec(mod_block, per_batch),
                  _const_spec((1, D_MODEL)),
                  _const_spec((D_MODEL, D_FF)),
                  _const_spec((D_MODEL, D_FF)),
                  _const_spec((N_FF, FF_TILE, D_MODEL))],
        out_specs=pl.BlockSpec((1, tm, D_MODEL), lambda b, i: (b, i, 0)),
        out_shape=jax.ShapeDtypeStruct(x.shape, jnp.float32),
        scratch_shapes=[pltpu.VMEM((tm, D_MODEL), jnp.bfloat16),
                        pltpu.VMEM((tm, D_MODEL), jnp.float32)],
        compiler_params=pltpu.CompilerParams(dimension_semantics=("arbitrary", "arbitrary"),
                                             vmem_limit_bytes=VMEM_LIMIT),
        name="ffn",
    )(x, mod, norm_w, w1, w3, w2)


def _lane_iota(shape):
    return lax.broadcasted_iota(jnp.int32, shape, len(shape) - 1)


def _sq_bf16(x):
    return (x * x).astype(jnp.bfloat16)


def _seg_rms(sq, sel_ref, n):
    ss = jnp.dot(sq, sel_ref[...], preferred_element_type=jnp.float32)
    return lax.rsqrt(ss * (1.0 / n) + EPS)


def _kv_prep(proj, kvn_ref, wukv_ref, wkn_ref, wkab_ref, tk_ref, selk_ref, k_ref, v_ref):
    tm = proj.shape[0]
    ckv = proj[:, EXT_KV:EXT_KV + KV_LORA]
    ckv = (ckv * _rms_scale(ckv, KV_LORA) * kvn_ref[...]).astype(jnp.bfloat16)
    kv = jnp.dot(ckv, wukv_ref[...], preferred_element_type=jnp.float32)
    lane = _lane_iota((tm, LANES))
    lo = lane < 64
    ab = proj[:, EXT_AB:EXT_AB + LANES]
    t = ab * (wkab_ref[...] * tk_ref[...])
    rope = t + pltpu.roll(t, 64, axis=1)
    ab_sq = _sq_bf16(ab)
    one_e = jnp.where(lane == V_HEAD, 1.0, 0.0)
    one_o = jnp.where(lane == 0, 1.0, 0.0)
    for p in range(N_PAIRS):
        vp = kv[:, MLA_HEADS * QK_NOPE + p * LANES:MLA_HEADS * QK_NOPE + (p + 1) * LANES]
        v_ref[0, :, (2 * p) * LANES:(2 * p + 1) * LANES] = jnp.where(lo, vp, one_e).astype(jnp.bfloat16)
        v_ref[0, :, (2 * p + 1) * LANES:(2 * p + 2) * LANES] = jnp.where(lo, one_o, vp).astype(jnp.bfloat16)
        kp = kv[:, p * LANES:(p + 1) * LANES]
        r = _seg_rms(jnp.concatenate([_sq_bf16(kp), ab_sq], axis=1), selk_ref, QK_HEAD)
        kw = kp * wkn_ref[...]
        k_ref[0, 2 * p] = (jnp.where(lo, kw, rope) * r[:, :LANES]).astype(jnp.bfloat16)
        k_ref[0, 2 * p + 1] = (jnp.where(lo, rope, kw) * r[:, LANES:]).astype(jnp.bfloat16)


def _modulated_proj(x, mod_ref, nw_ref, win_ref):
    shift, scale = mod_ref[0, 0], mod_ref[0, 1]
    h = (x * _rms_scale(x, D_MODEL) * (nw_ref[...] * (1.0 + scale)) + shift).astype(jnp.bfloat16)
    return jnp.dot(h, win_ref[...], preferred_element_type=jnp.float32)


def _prep_kernel(x_ref, mod_ref, nw_ref, win_ref, kvn_ref, wukv_ref, wkn_ref, wkab_ref, tk_ref,
                 selk_ref, qan_ref, wuq_ref, wqn_ref, tq_ref, selq_ref, vnw_ref, selv_ref,
                 ws_ref, bs_ref, k_ref, v_ref, q_ref, sg_ref):
    tm = x_ref.shape[1]
    proj = _modulated_proj(x_ref[0], mod_ref, nw_ref, win_ref)
    _kv_prep(proj, kvn_ref, wukv_ref, wkn_ref, wkab_ref, tk_ref, selk_ref, k_ref, v_ref)

    cq = proj[:, EXT_Q:EXT_Q + Q_LORA]
    cq = (cq * _rms_scale(cq, Q_LORA) * qan_ref[...]).astype(jnp.bfloat16)
    qall = jnp.dot(cq, wuq_ref[...], preferred_element_type=jnp.float32)
    lane = _lane_iota((tm, LANES))
    tabs = [wqn_ref[:, par * LANES:(par + 1) * LANES] * tq_ref[:, par * LANES:(par + 1) * LANES]
            * np.float32(QK_HEAD ** -0.5 * np.log2(np.e)) for par in range(2)]
    for p in range(N_PAIRS):
        qp = qall[:, 2 * p * LANES:(2 * p + 2) * LANES]
        r = _seg_rms(_sq_bf16(qp), selq_ref, QK_HEAD)
        for par in range(2):
            hl = slice(par * LANES, (par + 1) * LANES)
            q_ref[0, 2 * p + par] = (qp[:, hl] * r[:, hl] * tabs[par]).astype(jnp.bfloat16)

    u = _gelu_tanh(proj[:, EXT_U:EXT_U + GMLP_WIDTH])
    v = _gelu_tanh(proj[:, EXT_V:EXT_V + GMLP_WIDTH])
    lo = lane < 64
    vn_tiles = []
    for p in range(GMLP_GROUPS // 2):
        if p % 2 == 0:
            v2 = v[:, p * LANES:(p + 2) * LANES]
            vn2 = (v2 * _seg_rms(_sq_bf16(v2), selv_ref, GMLP_GROUP_DIM)
                   * vnw_ref[:, p * LANES:(p + 2) * LANES])
        vn = vn2[:, (p % 2) * LANES:(p % 2 + 1) * LANES]
        vn_tiles.append((jnp.where(lo, vn, 0.0).astype(jnp.bfloat16),
                         jnp.where(lo, 0.0, vn).astype(jnp.bfloat16)))
    for c in range(tm // CHUNK):
        rows = slice(c * CHUNK, (c + 1) * CHUNK)
        for p in range(GMLP_GROUPS // 2):
            vblk = jnp.concatenate([vn_tiles[p][0][rows], vn_tiles[p][1][rows]], axis=0)
            s = (jnp.dot(ws_ref[p], vblk, preferred_element_type=jnp.float32)
                 + bs_ref[:, p * LANES:(p + 1) * LANES])
            sg_ref[0, rows, p * LANES:(p + 1) * LANES] = (
                u[rows, p * LANES:(p + 1) * LANES] * s).astype(jnp.bfloat16)


def _kvonly_kernel(x_ref, mod_ref, nw_ref, win_ref, kvn_ref, wukv_ref, wkn_ref, wkab_ref, tk_ref,
                   selk_ref, k_ref, v_ref):
    proj = _modulated_proj(x_ref[0], mod_ref, nw_ref, win_ref)
    _kv_prep(proj, kvn_ref, wukv_ref, wkn_ref, wkab_ref, tk_ref, selk_ref, k_ref, v_ref)


def _prep_call(x, mod, per_batch, wts, tabs, tm, with_q):
    bsz, seq, _ = x.shape
    x_spec = pl.BlockSpec((1, tm, D_MODEL), lambda b, i: (b, i, 0))
    tab_spec = lambda w: pl.BlockSpec((tm, w), lambda b, i: (i, 0))
    ncols = EXT_COLS if with_q else EXT_KV_ONLY
    kv_specs = [x_spec, _mod_spec(1, per_batch), _const_spec((1, D_MODEL)),
                _const_spec((D_MODEL, ncols)), _const_spec((1, KV_LORA)),
                _const_spec((KV_LORA, MLA_HEADS * LANES)), _const_spec((1, LANES)),
                _const_spec((1, LANES)), tab_spec(LANES), _const_spec((2 * LANES, 2 * LANES))]
    kv_args = [x, mod, wts["norm2"], wts["w_in"] if with_q else wts["w_in"][:, :EXT_KV_ONLY],
               wts["kvn"], wts["w_ukv"], wts["wk_nope"], wts["wk_ab"], tabs["k"], wts["sel_k"]]
    k_shape = jax.ShapeDtypeStruct((bsz, MLA_HEADS, seq, HEAD_PAD), jnp.bfloat16)
    v_shape = jax.ShapeDtypeStruct((bsz, seq, MLA_HEADS * LANES), jnp.bfloat16)
    sg_shape = jax.ShapeDtypeStruct((bsz, seq, GMLP_WIDTH), jnp.bfloat16)
    k_spec = pl.BlockSpec((1, MLA_HEADS, tm, HEAD_PAD), lambda b, i: (b, 0, i, 0))
    v_spec = pl.BlockSpec((1, tm, MLA_HEADS * LANES), lambda b, i: (b, i, 0))
    sg_spec = pl.BlockSpec((1, tm, GMLP_WIDTH), lambda b, i: (b, i, 0))
    params = pltpu.CompilerParams(dimension_semantics=("arbitrary", "arbitrary"),
                                  vmem_limit_bytes=VMEM_LIMIT)
    if not with_q:
        return pl.pallas_call(
            _kvonly_kernel, grid=(bsz, seq // tm), in_specs=kv_specs,
            out_specs=[k_spec, v_spec], out_shape=[k_shape, v_shape],
            compiler_params=params, name="kv_prep")(*kv_args)
    q_specs = [_const_spec((1, Q_LORA)), _const_spec((Q_LORA, MLA_HEADS * LANES)),
               _const_spec((1, 2 * LANES)), tab_spec(2 * LANES), _const_spec((2 * LANES, 2 * LANES)),
               _const_spec((1, GMLP_WIDTH)), _const_spec((2 * LANES, 2 * LANES)),
               _const_spec((GMLP_GROUPS // 2, CHUNK, 2 * CHUNK)), _const_spec((CHUNK, GMLP_WIDTH))]
    q_args = [wts["qan"], wts["w_uq"], wts["wq"], tabs["q"], wts["sel_q"], wts["vnw"], wts["sel_v"],
              wts["ws"], wts["bs"]]
    return pl.pallas_call(
        _prep_kernel, grid=(bsz, seq // tm), in_specs=kv_specs + q_specs,
        out_specs=[k_spec, v_spec, k_spec, sg_spec],
        out_shape=[k_shape, v_shape, k_shape, sg_shape],
        compiler_params=params, name="mix_prep")(*kv_args, *q_args)


def _attn_kernel(q_ref, kl_ref, kc_ref, vl_ref, vc_ref, o_ref):
    for r0 in range(0, q_ref.shape[2], Q_SUB):
        _attn_rows(slice(r0, r0 + Q_SUB), q_ref, kl_ref, kc_ref, vl_ref, vc_ref, o_ref)


def _attn_rows(rows, q_ref, kl_ref, kc_ref, vl_ref, vc_ref, o_ref):
    nt = (((1,), (1,)), ((), ()))
    lane = _lane_iota((Q_SUB, LANES))
    lo = lane < 64
    ones_lane = (lane == V_HEAD, lane == 0)
    outs = []
    for h in range(MLA_HEADS):
        q = q_ref[0, h, rows]
        par = h % 2
        pv = slice((h - par) * LANES, (h - par + 2) * LANES)
        s1 = lax.dot_general(q, kl_ref[0, h], nt, preferred_element_type=jnp.float32)
        s2 = lax.dot_general(q, kc_ref[0, h], nt, preferred_element_type=jnp.float32)
        m = jnp.maximum(jnp.max(s1, axis=-1, keepdims=True), jnp.max(s2, axis=-1, keepdims=True))
        p1 = jnp.exp2(s1 - m).astype(jnp.bfloat16)
        p2 = jnp.exp2(s2 - m).astype(jnp.bfloat16)
        o = (jnp.dot(p1, vl_ref[0, :, pv], preferred_element_type=jnp.float32)
             + jnp.dot(p2, vc_ref[0, :, pv], preferred_element_type=jnp.float32))
        o = o[:, par * LANES:(par + 1) * LANES]
        l = jnp.sum(jnp.where(ones_lane[par], o, 0.0), axis=-1, keepdims=True)
        outs.append(o / l)
    for p in range(N_PAIRS):
        o_ref[0, rows, p * LANES:(p + 1) * LANES] = jnp.where(
            lo, outs[2 * p], outs[2 * p + 1]).astype(jnp.bfloat16)


def _attn_call(q, k_lat, k_ctx, v_lat, v_ctx, tq):
    bsz, _, seq, _ = q.shape
    n_ctx = k_ctx.shape[2]
    return pl.pallas_call(
        _attn_kernel,
        grid=(bsz, seq // tq),
        in_specs=[pl.BlockSpec((1, MLA_HEADS, tq, HEAD_PAD), lambda b, i: (b, 0, i, 0)),
                  pl.BlockSpec((1, MLA_HEADS, seq, HEAD_PAD), lambda b, i: (b, 0, 0, 0)),
                  pl.BlockSpec((1, MLA_HEADS, n_ctx, HEAD_PAD), lambda b, i: (b, 0, 0, 0)),
                  pl.BlockSpec((1, seq, MLA_HEADS * LANES), lambda b, i: (b, 0, 0)),
                  pl.BlockSpec((1, n_ctx, MLA_HEADS * LANES), lambda b, i: (b, 0, 0))],
        out_specs=pl.BlockSpec((1, tq, MLA_HEADS * V_HEAD), lambda b, i: (b, i, 0)),
        out_shape=jax.ShapeDtypeStruct((bsz, seq, MLA_HEADS * V_HEAD), jnp.bfloat16),
        compiler_params=pltpu.CompilerParams(
            dimension_semantics=("arbitrary", "arbitrary"),
            vmem_limit_bytes=VMEM_LIMIT),
        name="attention",
    )(q, k_lat, k_ctx, v_lat, v_ctx)


def _out_ffn_kernel(x_ref, attn_ref, sg_ref, modm_ref, wout_ref, mod_ref, nw_ref,
                    w1_ref, w3_ref, w2_ref, o_ref, h_ref, acc_ref):
    y = (jnp.dot(attn_ref[0], wout_ref[0], preferred_element_type=jnp.float32)
         + jnp.dot(sg_ref[0], wout_ref[1], preferred_element_type=jnp.float32))
    x = x_ref[0] + modm_ref[0, 2] * y
    o_ref[0] = _ffn_core(x, mod_ref, nw_ref, w1_ref, w3_ref, w2_ref, h_ref, acc_ref)


def _out_ffn_call(x, attn, sg, mod, w_out, norm_w, w1, w3, w2, tm):
    bsz, seq, _ = x.shape
    half = MLA_HEADS * V_HEAD
    row = lambda w: pl.BlockSpec((1, tm, w), lambda b, i: (b, i, 0))
    return pl.pallas_call(
        _out_ffn_kernel,
        grid=(bsz, seq // tm),
        in_specs=[row(D_MODEL), row(half), row(GMLP_WIDTH), _mod_spec(1, True),
                  _const_spec((2, half, D_MODEL)), _mod_spec(2, True), _const_spec((1, D_MODEL)),
                  _const_spec((D_MODEL, D_FF)), _const_spec((D_MODEL, D_FF)),
                  _const_spec((N_FF, FF_TILE, D_MODEL))],
        out_specs=row(D_MODEL),
        out_shape=jax.ShapeDtypeStruct(x.shape, jnp.float32),
        scratch_shapes=[pltpu.VMEM((tm, D_MODEL), jnp.bfloat16),
                        pltpu.VMEM((tm, D_MODEL), jnp.float32)],
        compiler_params=pltpu.CompilerParams(dimension_semantics=("arbitrary", "arbitrary"),
                                             vmem_limit_bytes=VMEM_LIMIT),
        name="out_ffn",
    )(x, attn, sg, mod, w_out, mod, norm_w, w1, w3, w2)


def _ffn_weights(w1, w3, w2):
    bf = jnp.bfloat16
    return w1.astype(bf), w3.astype(bf), w2.astype(bf).reshape(N_FF, FF_TILE, D_MODEL)


def _rot_cols(w, start, signed=True):
    half = AXIS_DIM // 2
    parts = []
    for blk in range(QK_ROPE // half):
        src = start + (blk + 1) * half if blk % 2 == 0 else start + (blk - 1) * half
        piece = w[..., src:src + half]
        parts.append(-piece if (signed and blk % 2 == 0) else piece)
    return jnp.concatenate(parts, axis=-1)


def _selectors():
    r = np.arange(2 * LANES)[:, None]
    c = np.arange(2 * LANES)[None, :]
    sel_k = ((r < 64) & (c < LANES)) | ((r >= 64) & (r < LANES) & (c >= LANES)) | (
        (r >= LANES) & (r < LANES + QK_ROPE))
    ro = r - LANES
    sel_q = ((r < QK_HEAD) & (c < LANES)) | (
        (r >= LANES) & ((ro < QK_ROPE) | (ro >= 2 * QK_ROPE)) & (c >= LANES))
    sel_v = (r // GMLP_GROUP_DIM) == (c // GMLP_GROUP_DIM)
    return [jnp.asarray(m, jnp.bfloat16) for m in (sel_k, sel_q, sel_v)]


def _mix_weights(norm2_w, w_in, q_a_norm_w, w_uq, kv_a_norm_w, w_ukv, q_norm_w, k_norm_w,
                 v_norm_w, w_s, b_s):
    bf = jnp.bfloat16
    kpe = w_in[:, KV_LORA:KV_COLS]
    kpe_rot = _rot_cols(w_in, KV_LORA)
    w_in_ext = jnp.concatenate([w_in[:, :KV_LORA], kpe, kpe, kpe_rot, kpe_rot,
                                w_in[:, Q_START:]], axis=1).astype(bf)
    ukv = w_ukv.reshape(KV_LORA, MLA_HEADS, QK_NOPE + V_HEAD)
    w_ukv_p = jnp.concatenate([ukv[:, :, :QK_NOPE].reshape(KV_LORA, -1),
                               ukv[:, :, QK_NOPE:].reshape(KV_LORA, -1)], axis=1).astype(bf)
    cols = []
    for h in range(MLA_HEADS):
        base = h * QK_HEAD
        nope = w_uq[:, base:base + QK_NOPE]
        pes = w_uq[:, base + QK_NOPE:base + QK_HEAD]
        rots = _rot_cols(w_uq, base + QK_NOPE)
        cols += [nope, pes, rots] if h % 2 == 0 else [pes, rots, nope]
    w_uq_ext = jnp.concatenate(cols, axis=1).astype(bf)
    qn, qp, qr = q_norm_w[:QK_NOPE], q_norm_w[QK_NOPE:], _rot_cols(q_norm_w, QK_NOPE, signed=False)
    wq = jnp.concatenate([qn, qp, qr, qp, qr, qn])[None]
    kn, kp, kr = k_norm_w[:QK_NOPE], k_norm_w[QK_NOPE:], _rot_cols(k_norm_w, QK_NOPE, signed=False)
    sel_k, sel_q, sel_v = _selectors()
    return dict(
        sel_k=sel_k, sel_q=sel_q, sel_v=sel_v,
        norm2=norm2_w[None], w_in=w_in_ext, kvn=kv_a_norm_w[None], w_ukv=w_ukv_p,
        wk_nope=jnp.concatenate([kn, kn])[None],
        wk_ab=jnp.concatenate([kp, kp, kr, kr])[None],
        qan=q_a_norm_w[None], w_uq=w_uq_ext, wq=wq,
        vnw=v_norm_w.reshape(1, GMLP_WIDTH),
        ws=w_s.astype(bf).reshape(GMLP_GROUPS // 2, 2, CHUNK, CHUNK).transpose(0, 2, 1, 3
                                   ).reshape(GMLP_GROUPS // 2, CHUNK, 2 * CHUNK),
        bs=jnp.broadcast_to(b_s.T[:, :, None], (CHUNK, GMLP_GROUPS, GMLP_GROUP_DIM)
                            ).reshape(CHUNK, GMLP_WIDTH),
    )


def _rope_tables(seq, n_ctx):
    f32 = jnp.float32
    rows_n = seq // GRID_W
    rows = jnp.repeat(jnp.arange(rows_n, dtype=f32), GRID_W)
    cols = jnp.tile(jnp.arange(GRID_W, dtype=f32), rows_n)
    inv = ROPE_BASE ** (-jnp.arange(0, AXIS_DIM, 2, dtype=f32) / AXIS_DIM)
    ang_r = rows[:, None] * inv
    ang_c = cols[:, None] * inv
    ang = jnp.concatenate([ang_r, ang_r, ang_c, ang_c], axis=-1)
    cos, sin = jnp.cos(ang), jnp.sin(ang)
    one = jnp.ones((seq, 64), f32)
    lat = dict(k=jnp.concatenate([cos, cos, sin, sin], 1),
               q=jnp.concatenate([one, cos, sin, cos, sin, one], 1))
    ctx = dict(k=jnp.concatenate([jnp.ones((n_ctx, 64), f32), jnp.zeros((n_ctx, 64), f32)], 1))
    return lat, ctx


def kernel(x, c, ctx, c_ctx, w_ada, b_ada, norm1_w, ffn1_w1, ffn1_w3, ffn1_w2, norm2_w, w_in,
           q_a_norm_w, w_uq, kv_a_norm_w, w_ukv, q_norm_w, k_norm_w, v_norm_w, w_s, b_s, w_out,
           norm3_w, ffn2_w1, ffn2_w3, ffn2_w2):
    bsz, seq, _ = x.shape
    n_ctx = ctx.shape[1]
    rows = -(-(bsz + 1) // 8) * 8
    cc = jnp.concatenate([c, c_ctx[None], jnp.zeros((rows - bsz - 1, D_MODEL), jnp.float32)], 0)
    mod = _ada_call(cc, w_ada[0], b_ada[0][None]).reshape(rows, N_MOD, 1, D_MODEL)
    mod_ctx = mod[bsz:bsz + 1]

    f1 = _ffn_weights(ffn1_w1[0], ffn1_w3[0], ffn1_w2[0])
    f2 = _ffn_weights(ffn2_w1[0], ffn2_w3[0], ffn2_w2[0])
    wts = _mix_weights(norm2_w[0], w_in[0], q_a_norm_w[0], w_uq[0], kv_a_norm_w[0], w_ukv[0],
                       q_norm_w[0], k_norm_w[0], v_norm_w[0], w_s[0], b_s[0])
    tabs_lat, tabs_ctx = _rope_tables(seq, n_ctx)

    x1 = _ffn_call(x, mod, 0, True, norm1_w, *f1, tm=ROW_TILE)
    ctx1 = _ffn_call(ctx.reshape(1, bsz * n_ctx, D_MODEL), mod_ctx, 0, False, norm1_w, *f1,
                     tm=ROW_TILE).reshape(bsz, n_ctx, D_MODEL)
    k_lat, v_lat, q, sg = _prep_call(x1, mod, True, wts, tabs_lat, tm=ROW_TILE, with_q=True)
    k_ctx, v_ctx = _prep_call(ctx1, mod_ctx, False, wts, tabs_ctx, tm=n_ctx, with_q=False)
    attn = _attn_call(q, k_lat, k_ctx, v_lat, v_ctx, tq=Q_TILE)
    w_out_r = w_out[0].astype(jnp.bfloat16).reshape(2, MLA_HEADS * V_HEAD, D_MODEL)
    return _out_ffn_call(x1, attn, sg, mod, w_out_r, norm3_w, *f2, tm=ROW_TILE)
```

```python
import numpy as np
import jax
import jax.numpy as jnp
from jax import lax
from jax.experimental import pallas as pl
from jax.experimental.pallas import tpu as pltpu

D_MODEL = 1024
GRID_W = 64
MLA_HEADS = 8
QK_NOPE = 64
QK_ROPE = 32
QK_HEAD = QK_NOPE + QK_ROPE
V_HEAD = 64
Q_LORA = 256
KV_LORA = 128
AXIS_DIM = QK_ROPE // 2
ROPE_BASE = 10000.0
GMLP_GROUPS = 8
GMLP_GROUP_DIM = 64
GMLP_WIDTH = GMLP_GROUPS * GMLP_GROUP_DIM
CHUNK = 128
KV_COLS = KV_LORA + QK_ROPE
Q_START = KV_COLS
U_START = KV_COLS + Q_LORA
V_START = U_START + GMLP_WIDTH
IN_COLS = V_START + GMLP_WIDTH
D_FF = 2816
N_MOD = 9
EPS = 1e-6

LANES = 128
HEAD_PAD = LANES
N_PAIRS = MLA_HEADS // 2
VMEM_LIMIT = 56 * 1024 * 1024

EXT_KV = 0
EXT_AB = 128
EXT_Q = 256
EXT_U = EXT_Q + Q_LORA
EXT_V = EXT_U + GMLP_WIDTH
EXT_COLS = EXT_V + GMLP_WIDTH
EXT_KV_ONLY = EXT_Q

FF_TILE = 256
N_FF = D_FF // FF_TILE
ROW_TILE = 1024
PREP_TILE = 512
Q_TILE = 1024
Q_SUB = 512


def _rms_scale(x, n):
    return lax.rsqrt(jnp.sum(x * x, axis=-1, keepdims=True) * (1.0 / n) + EPS)


def _silu(a):
    return a / (1.0 + jnp.exp(-a))


def _gelu_tanh(x):
    c = np.float32(np.sqrt(2.0 / np.pi))
    t = jnp.tanh(x * (c + np.float32(c * 0.044715) * (x * x)))
    return x * (0.5 + 0.5 * t)


def _ada_kernel(c_ref, w_ref, b_ref, o_ref):
    s = _silu(c_ref[...]).astype(jnp.bfloat16)
    o_ref[...] = jnp.dot(s, w_ref[...].astype(jnp.bfloat16),
                         preferred_element_type=jnp.float32) + b_ref[...]


def _ada_call(cc, w_ada, b_ada):
    rows = cc.shape[0]
    n = w_ada.shape[1]
    tn = 1024
    return pl.pallas_call(
        _ada_kernel,
        grid=(n // tn,),
        in_specs=[pl.BlockSpec((rows, D_MODEL), lambda j: (0, 0)),
                  pl.BlockSpec((D_MODEL, tn), lambda j: (0, j)),
                  pl.BlockSpec((1, tn), lambda j: (0, j))],
        out_specs=pl.BlockSpec((rows, tn), lambda j: (0, j)),
        out_shape=jax.ShapeDtypeStruct((rows, n), jnp.float32),
        compiler_params=pltpu.CompilerParams(dimension_semantics=("arbitrary",),
                                             vmem_limit_bytes=VMEM_LIMIT),
        name="adaln",
    )(cc, w_ada, b_ada)


def _ffn_core(x, mod_ref, nw_ref, w1_ref, w3_ref, w2_ref, h_ref, acc_ref):
    shift, scale, gate = mod_ref[0, 0], mod_ref[0, 1], mod_ref[0, 2]
    h_ref[...] = (x * _rms_scale(x, D_MODEL) * (nw_ref[...] * (1.0 + scale)) + shift
                  ).astype(jnp.bfloat16)
    acc_ref[...] = jnp.zeros_like(acc_ref)

    for j in range(N_FF):
        hb = h_ref[...]
        cols = slice(j * FF_TILE, (j + 1) * FF_TILE)
        a = jnp.dot(hb, w1_ref[:, cols], preferred_element_type=jnp.float32)
        b = jnp.dot(hb, w3_ref[:, cols], preferred_element_type=jnp.float32)
        g = (_silu(a) * b).astype(jnp.bfloat16)
        acc_ref[...] += jnp.dot(g, w2_ref[j], preferred_element_type=jnp.float32)
    return x + (0.5 * gate) * acc_ref[...]


def _ffn_kernel(x_ref, mod_ref, nw_ref, w1_ref, w3_ref, w2_ref, o_ref, h_ref, acc_ref):
    o_ref[0] = _ffn_core(x_ref[0], mod_ref, nw_ref, w1_ref, w3_ref, w2_ref, h_ref, acc_ref)


def _const_spec(shape):
    nd = len(shape)
    return pl.BlockSpec(shape, lambda *_: (0,) * nd, pipeline_mode=pl.Buffered(1))


def _mod_spec(mod_block, per_batch):
    if per_batch:
        return pl.BlockSpec((1, 3, 1, D_MODEL), lambda b, i: (b, mod_block, 0, 0))
    return pl.BlockSpec((1, 3, 1, D_MODEL), lambda b, i: (0, mod_block, 0, 0))


def _ffn_call(x, mod, mod_block, per_batch, norm_w, w1, w3, w2, tm):
    bsz, seq, _ = x.shape
    assert seq % tm == 0, (seq, tm)
    return pl.pallas_call(
        _ffn_kernel,
        grid=(bsz, seq // tm),
        in_specs=[pl.BlockSpec((1, tm, D_MODEL), lambda b, i: (b, i, 0)),
                  _mod_spec(mod_block, per_batch),
                  _const_spec((1, D_MODEL)),
                  _const_spec((D_MODEL, D_FF)),
                  _const_spec((D_MODEL, D_FF)),
                  _const_spec((N_FF, FF_TILE, D_MODEL))],
        out_specs=pl.BlockSpec((1, tm, D_MODEL), lambda b, i: (b, i, 0)),
        out_shape=jax.ShapeDtypeStruct(x.shape, jnp.float32),
        scratch_shapes=[pltpu.VMEM((tm, D_MODEL), jnp.bfloat16),
                        pltpu.VMEM((tm, D_MODEL), jnp.float32)],
        compiler_params=pltpu.CompilerParams(dimension_semantics=("arbitrary", "arbitrary"),
                                             vmem_limit_bytes=VMEM_LIMIT),
        name="ffn",
    )(x, mod, norm_w, w1, w3, w2)


def _lane_iota(shape):
    return lax.broadcasted_iota(jnp.int32, shape, len(shape) - 1)


def _sq_bf16(x):
    return (x * x).astype(jnp.bfloat16)


def _seg_rms(sq, sel_ref, n):
    ss = jnp.dot(sq, sel_ref[...], preferred_element_type=jnp.float32)
    return lax.rsqrt(ss + n * EPS)


def _kv_prep(proj, kvn_ref, wukv_ref, wkn_ref, wkab_ref, tk_ref, selk_ref, k_ref, v_ref):
    tm = proj.shape[0]
    ckv = proj[:, EXT_KV:EXT_KV + KV_LORA]
    ckv = (ckv * _rms_scale(ckv, KV_LORA) * kvn_ref[...]).astype(jnp.bfloat16)
    kv = jnp.dot(ckv, wukv_ref[...], preferred_element_type=jnp.float32)
    lane = _lane_iota((tm, LANES))
    lo = lane < 64
    ab = proj[:, EXT_AB:EXT_AB + LANES]
    root_n = np.float32(np.sqrt(QK_HEAD))
    t = ab * ((wkab_ref[...] * root_n) * tk_ref[...])
    rope = t + pltpu.roll(t, 64, axis=1)
    ab_sq = _sq_bf16(ab)
    one_e = jnp.where(lane == V_HEAD, 1.0, 0.0)
    one_o = jnp.where(lane == 0, 1.0, 0.0)
    for p in range(N_PAIRS):
        vp = kv[:, MLA_HEADS * QK_NOPE + p * LANES:MLA_HEADS * QK_NOPE + (p + 1) * LANES]
        v_ref[0, :, (2 * p) * LANES:(2 * p + 1) * LANES] = jnp.where(lo, vp, one_e).astype(jnp.bfloat16)
        v_ref[0, :, (2 * p + 1) * LANES:(2 * p + 2) * LANES] = jnp.where(lo, one_o, vp).astype(jnp.bfloat16)
        kp = kv[:, p * LANES:(p + 1) * LANES]
        r = _seg_rms(jnp.concatenate([_sq_bf16(kp), ab_sq], axis=1), selk_ref, QK_HEAD)
        kw = kp * (wkn_ref[...] * root_n)
        k_ref[0, 2 * p] = (jnp.where(lo, kw, rope) * r[:, :LANES]).astype(jnp.bfloat16)
        k_ref[0, 2 * p + 1] = (jnp.where(lo, rope, kw) * r[:, LANES:]).astype(jnp.bfloat16)


def _modulated_proj(x, mod_ref, nw_ref, win_ref):
    shift, scale = mod_ref[0, 0], mod_ref[0, 1]
    h = (x * _rms_scale(x, D_MODEL) * (nw_ref[...] * (1.0 + scale)) + shift).astype(jnp.bfloat16)
    return jnp.dot(h, win_ref[...], preferred_element_type=jnp.float32)


def _prep_kernel(x_ref, mod_ref, nw_ref, win_ref, kvn_ref, wukv_ref, wkn_ref, wkab_ref, tk_ref,
                 selk_ref, qan_ref, wuq_ref, wqn_ref, tq_ref, selq_ref, vnw_ref, selv_ref,
                 ws_ref, bs_ref, k_ref, v_ref, q_ref, sg_ref):
    tm = x_ref.shape[1]
    proj = _modulated_proj(x_ref[0], mod_ref, nw_ref, win_ref)
    _kv_prep(proj, kvn_ref, wukv_ref, wkn_ref, wkab_ref, tk_ref, selk_ref, k_ref, v_ref)

    cq = proj[:, EXT_Q:EXT_Q + Q_LORA]
    cq = (cq * _rms_scale(cq, Q_LORA) * qan_ref[...]).astype(jnp.bfloat16)
    qall = jnp.dot(cq, wuq_ref[...], preferred_element_type=jnp.float32)
    lane = _lane_iota((tm, LANES))
    tabs = [wqn_ref[:, par * LANES:(par + 1) * LANES] * tq_ref[:, par * LANES:(par + 1) * LANES]
            * np.float32(np.log2(np.e)) for par in range(2)]
    for p in range(N_PAIRS):
        qp = qall[:, 2 * p * LANES:(2 * p + 2) * LANES]
        r = _seg_rms(_sq_bf16(qp), selq_ref, QK_HEAD)
        for par in range(2):
            hl = slice(par * LANES, (par + 1) * LANES)
            q_ref[0, 2 * p + par] = (qp[:, hl] * r[:, hl] * tabs[par]).astype(jnp.bfloat16)

    u = _gelu_tanh(proj[:, EXT_U:EXT_U + GMLP_WIDTH])
    v = _gelu_tanh(proj[:, EXT_V:EXT_V + GMLP_WIDTH])
    lo = lane < 64
    vn_tiles = []
    for p in range(GMLP_GROUPS // 2):
        if p % 2 == 0:
            v2 = v[:, p * LANES:(p + 2) * LANES]
            vn2 = (v2 * _seg_rms(_sq_bf16(v2), selv_ref, GMLP_GROUP_DIM)
                   * (vnw_ref[:, p * LANES:(p + 2) * LANES] * np.float32(np.sqrt(GMLP_GROUP_DIM))))
        vn = vn2[:, (p % 2) * LANES:(p % 2 + 1) * LANES]
        vn_tiles.append((jnp.where(lo, vn, 0.0).astype(jnp.bfloat16),
                         jnp.where(lo, 0.0, vn).astype(jnp.bfloat16)))
    for c in range(tm // CHUNK):
        rows = slice(c * CHUNK, (c + 1) * CHUNK)
        for p in range(GMLP_GROUPS // 2):
            vblk = jnp.concatenate([vn_tiles[p][0][rows], vn_tiles[p][1][rows]], axis=0)
            s = (jnp.dot(ws_ref[p], vblk, preferred_element_type=jnp.float32)
                 + bs_ref[:, p * LANES:(p + 1) * LANES])
            sg_ref[0, rows, p * LANES:(p + 1) * LANES] = (
                u[rows, p * LANES:(p + 1) * LANES] * s).astype(jnp.bfloat16)


def _kvonly_kernel(x_ref, mod_ref, nw_ref, win_ref, kvn_ref, wukv_ref, wkn_ref, wkab_ref, tk_ref,
                   selk_ref, k_ref, v_ref):
    proj = _modulated_proj(x_ref[0], mod_ref, nw_ref, win_ref)
    _kv_prep(proj, kvn_ref, wukv_ref, wkn_ref, wkab_ref, tk_ref, selk_ref, k_ref, v_ref)


def _prep_call(x, mod, per_batch, wts, tabs, tm, with_q):
    bsz, seq, _ = x.shape
    assert seq % tm == 0 and tm % CHUNK == 0, (seq, tm)
    x_spec = pl.BlockSpec((1, tm, D_MODEL), lambda b, i: (b, i, 0))
    tab_spec = lambda w: pl.BlockSpec((tm, w), lambda b, i: (i, 0))
    ncols = EXT_COLS if with_q else EXT_KV_ONLY
    kv_specs = [x_spec, _mod_spec(1, per_batch), _const_spec((1, D_MODEL)),
                _const_spec((D_MODEL, ncols)), _const_spec((1, KV_LORA)),
                _const_spec((KV_LORA, MLA_HEADS * LANES)), _const_spec((1, LANES)),
                _const_spec((1, LANES)), tab_spec(LANES), _const_spec((2 * LANES, 2 * LANES))]
    kv_args = [x, mod, wts["norm2"], wts["w_in"] if with_q else wts["w_in"][:, :EXT_KV_ONLY],
               wts["kvn"], wts["w_ukv"], wts["wk_nope"], wts["wk_ab"], tabs["k"], wts["sel_k"]]
    k_shape = jax.ShapeDtypeStruct((bsz, MLA_HEADS, seq, HEAD_PAD), jnp.bfloat16)
    v_shape = jax.ShapeDtypeStruct((bsz, seq, MLA_HEADS * LANES), jnp.bfloat16)
    sg_shape = jax.ShapeDtypeStruct((bsz, seq, GMLP_WIDTH), jnp.bfloat16)
    k_spec = pl.BlockSpec((1, MLA_HEADS, tm, HEAD_PAD), lambda b, i: (b, 0, i, 0))
    v_spec = pl.BlockSpec((1, tm, MLA_HEADS * LANES), lambda b, i: (b, i, 0))
    sg_spec = pl.BlockSpec((1, tm, GMLP_WIDTH), lambda b, i: (b, i, 0))
    params = pltpu.CompilerParams(dimension_semantics=("arbitrary", "arbitrary"),
                                  vmem_limit_bytes=VMEM_LIMIT)
    if not with_q:
        return pl.pallas_call(
            _kvonly_kernel, grid=(bsz, seq // tm), in_specs=kv_specs,
            out_specs=[k_spec, v_spec], out_shape=[k_shape, v_shape],
            compiler_params=params, name="kv_prep")(*kv_args)
    q_specs = [_const_spec((1, Q_LORA)), _const_spec((Q_LORA, MLA_HEADS * LANES)),
               _const_spec((1, 2 * LANES)), tab_spec(2 * LANES), _const_spec((2 * LANES, 2 * LANES)),
               _const_spec((1, GMLP_WIDTH)), _const_spec((2 * LANES, 2 * LANES)),
               _const_spec((GMLP_GROUPS // 2, CHUNK, 2 * CHUNK)), _const_spec((CHUNK, GMLP_WIDTH))]
    q_args = [wts["qan"], wts["w_uq"], wts["wq"], tabs["q"], wts["sel_q"], wts["vnw"], wts["sel_v"],
              wts["ws"], wts["bs"]]
    return pl.pallas_call(
        _prep_kernel, grid=(bsz, seq // tm), in_specs=kv_specs + q_specs,
        out_specs=[k_spec, v_spec, k_spec, sg_spec],
        out_shape=[k_shape, v_shape, k_shape, sg_shape],
        compiler_params=params, name="mix_prep")(*kv_args, *q_args)


def _attn_kernel(q_ref, kl_ref, kc_ref, vl_ref, vc_ref, o_ref):
    for r0 in range(0, q_ref.shape[2], Q_SUB):
        _attn_rows(slice(r0, r0 + Q_SUB), q_ref, kl_ref, kc_ref, vl_ref, vc_ref, o_ref)


def _attn_rows(rows, q_ref, kl_ref, kc_ref, vl_ref, vc_ref, o_ref):
    nt = (((1,), (1,)), ((), ()))
    lane = _lane_iota((Q_SUB, LANES))
    lo = lane < 64
    ones_lane = (lane == V_HEAD, lane == 0)
    outs = []
    for h in range(MLA_HEADS):
        q = q_ref[0, h, rows]
        par = h % 2
        pv = slice((h - par) * LANES, (h - par + 2) * LANES)
        s1 = lax.dot_general(q, kl_ref[0, h], nt, preferred_element_type=jnp.float32)
        s2 = lax.dot_general(q, kc_ref[0, h], nt, preferred_element_type=jnp.float32)
        m = jnp.maximum(jnp.max(s1, axis=-1, keepdims=True), jnp.max(s2, axis=-1, keepdims=True))
        p1 = jnp.exp2(s1 - m).astype(jnp.bfloat16)
        p2 = jnp.exp2(s2 - m).astype(jnp.bfloat16)
        o = (jnp.dot(p1, vl_ref[0, :, pv], preferred_element_type=jnp.float32)
             + jnp.dot(p2, vc_ref[0, :, pv], preferred_element_type=jnp.float32))
        o = o[:, par * LANES:(par + 1) * LANES]
        l = jnp.sum(jnp.where(ones_lane[par], o, 0.0), axis=-1, keepdims=True)
        outs.append(o / l)
    for p in range(N_PAIRS):
        o_ref[0, rows, p * LANES:(p + 1) * LANES] = jnp.where(
            lo, outs[2 * p], outs[2 * p + 1]).astype(jnp.bfloat16)


def _attn_call(q, k_lat, k_ctx, v_lat, v_ctx, tq):
    bsz, _, seq, _ = q.shape
    assert seq % tq == 0 and tq % Q_SUB == 0, (seq, tq)
    n_ctx = k_ctx.shape[2]
    return pl.pallas_call(
        _attn_kernel,
        grid=(bsz, seq // tq),
        in_specs=[pl.BlockSpec((1, MLA_HEADS, tq, HEAD_PAD), lambda b, i: (b, 0, i, 0)),
                  pl.BlockSpec((1, MLA_HEADS, seq, HEAD_PAD), lambda b, i: (b, 0, 0, 0)),
                  pl.BlockSpec((1, MLA_HEADS, n_ctx, HEAD_PAD), lambda b, i: (b, 0, 0, 0)),
                  pl.BlockSpec((1, seq, MLA_HEADS * LANES), lambda b, i: (b, 0, 0)),
                  pl.BlockSpec((1, n_ctx, MLA_HEADS * LANES), lambda b, i: (b, 0, 0))],
        out_specs=pl.BlockSpec((1, tq, MLA_HEADS * V_HEAD), lambda b, i: (b, i, 0)),
        out_shape=jax.ShapeDtypeStruct((bsz, seq, MLA_HEADS * V_HEAD), jnp.bfloat16),
        compiler_params=pltpu.CompilerParams(
            dimension_semantics=("arbitrary", "arbitrary"),
            vmem_limit_bytes=VMEM_LIMIT),
        name="attention",
    )(q, k_lat, k_ctx, v_lat, v_ctx)


def _out_ffn_kernel(x_ref, attn_ref, sg_ref, modm_ref, wout_ref, mod_ref, nw_ref,
                    w1_ref, w3_ref, w2_ref, o_ref, h_ref, acc_ref):
    y = (jnp.dot(attn_ref[0], wout_ref[0], preferred_element_type=jnp.float32)
         + jnp.dot(sg_ref[0], wout_ref[1], preferred_element_type=jnp.float32))
    x = x_ref[0] + modm_ref[0, 2] * y
    o_ref[0] = _ffn_core(x, mod_ref, nw_ref, w1_ref, w3_ref, w2_ref, h_ref, acc_ref)


def _out_ffn_call(x, attn, sg, mod, w_out, norm_w, w1, w3, w2, tm):
    bsz, seq, _ = x.shape
    assert seq % tm == 0, (seq, tm)
    half = MLA_HEADS * V_HEAD
    row = lambda w: pl.BlockSpec((1, tm, w), lambda b, i: (b, i, 0))
    return pl.pallas_call(
        _out_ffn_kernel,
        grid=(bsz, seq // tm),
        in_specs=[row(D_MODEL), row(half), row(GMLP_WIDTH), _mod_spec(1, True),
                  _const_spec((2, half, D_MODEL)), _mod_spec(2, True), _const_spec((1, D_MODEL)),
                  _const_spec((D_MODEL, D_FF)), _const_spec((D_MODEL, D_FF)),
                  _const_spec((N_FF, FF_TILE, D_MODEL))],
        out_specs=row(D_MODEL),
        out_shape=jax.ShapeDtypeStruct(x.shape, jnp.float32),
        scratch_shapes=[pltpu.VMEM((tm, D_MODEL), jnp.bfloat16),
                        pltpu.VMEM((tm, D_MODEL), jnp.float32)],
        compiler_params=pltpu.CompilerParams(dimension_semantics=("arbitrary", "arbitrary"),
                                             vmem_limit_bytes=VMEM_LIMIT),
        name="out_ffn",
    )(x, attn, sg, mod, w_out, mod, norm_w, w1, w3, w2)


def _ffn_weights(w1, w3, w2):
    bf = jnp.bfloat16
    return w1.astype(bf), w3.astype(bf), w2.astype(bf).reshape(N_FF, FF_TILE, D_MODEL)


def _rot_cols(w, start, signed=True):
    half = AXIS_DIM // 2
    parts = []
    for blk in range(QK_ROPE // half):
        src = start + (blk + 1) * half if blk % 2 == 0 else start + (blk - 1) * half
        piece = w[..., src:src + half]
        parts.append(-piece if (signed and blk % 2 == 0) else piece)
    return jnp.concatenate(parts, axis=-1)


def _selectors():
    r = np.arange(2 * LANES)[:, None]
    c = np.arange(2 * LANES)[None, :]
    sel_k = ((r < 64) & (c < LANES)) | ((r >= 64) & (r < LANES) & (c >= LANES)) | (
        (r >= LANES) & (r < LANES + QK_ROPE))
    ro = r - LANES
    sel_q = ((r < QK_HEAD) & (c < LANES)) | (
        (r >= LANES) & ((ro < QK_ROPE) | (ro >= 2 * QK_ROPE)) & (c >= LANES))
    sel_v = (r // GMLP_GROUP_DIM) == (c // GMLP_GROUP_DIM)
    return [jnp.asarray(m, jnp.bfloat16) for m in (sel_k, sel_q, sel_v)]


def _mix_weights(norm2_w, w_in, q_a_norm_w, w_uq, kv_a_norm_w, w_ukv, q_norm_w, k_norm_w,
                 v_norm_w, w_s, b_s):
    bf = jnp.bfloat16
    kpe = w_in[:, KV_LORA:KV_COLS]
    kpe_rot = _rot_cols(w_in, KV_LORA)
    w_in_ext = jnp.concatenate([w_in[:, :KV_LORA], kpe, kpe, kpe_rot, kpe_rot,
                                w_in[:, Q_START:]], axis=1).astype(bf)
    ukv = w_ukv.reshape(KV_LORA, MLA_HEADS, QK_NOPE + V_HEAD)
    w_ukv_p = jnp.concatenate([ukv[:, :, :QK_NOPE].reshape(KV_LORA, -1),
                               ukv[:, :, QK_NOPE:].reshape(KV_LORA, -1)], axis=1).astype(bf)
    cols = []
    for h in range(MLA_HEADS):
        base = h * QK_HEAD
        nope = w_uq[:, base:base + QK_NOPE]
        pes = w_uq[:, base + QK_NOPE:base + QK_HEAD]
        rots = _rot_cols(w_uq, base + QK_NOPE)
        cols += [nope, pes, rots] if h % 2 == 0 else [pes, rots, nope]
    w_uq_ext = jnp.concatenate(cols, axis=1).astype(bf)
    qn, qp, qr = q_norm_w[:QK_NOPE], q_norm_w[QK_NOPE:], _rot_cols(q_norm_w, QK_NOPE, signed=False)
    wq = jnp.concatenate([qn, qp, qr, qp, qr, qn])[None]
    kn, kp, kr = k_norm_w[:QK_NOPE], k_norm_w[QK_NOPE:], _rot_cols(k_norm_w, QK_NOPE, signed=False)
    sel_k, sel_q, sel_v = _selectors()
    return dict(
        sel_k=sel_k, sel_q=sel_q, sel_v=sel_v,
        norm2=norm2_w[None], w_in=w_in_ext, kvn=kv_a_norm_w[None], w_ukv=w_ukv_p,
        wk_nope=jnp.concatenate([kn, kn])[None],
        wk_ab=jnp.concatenate([kp, kp, kr, kr])[None],
        qan=q_a_norm_w[None], w_uq=w_uq_ext, wq=wq,
        vnw=v_norm_w.reshape(1, GMLP_WIDTH),
        ws=w_s.astype(bf).reshape(GMLP_GROUPS // 2, 2, CHUNK, CHUNK).transpose(0, 2, 1, 3
                                   ).reshape(GMLP_GROUPS // 2, CHUNK, 2 * CHUNK),
        bs=jnp.broadcast_to(b_s.T[:, :, None], (CHUNK, GMLP_GROUPS, GMLP_GROUP_DIM)
                            ).reshape(CHUNK, GMLP_WIDTH),
    )


def _rope_tables(seq, n_ctx):
    f32 = jnp.float32
    rows_n = seq // GRID_W
    rows = jnp.repeat(jnp.arange(rows_n, dtype=f32), GRID_W)
    cols = jnp.tile(jnp.arange(GRID_W, dtype=f32), rows_n)
    inv = ROPE_BASE ** (-jnp.arange(0, AXIS_DIM, 2, dtype=f32) / AXIS_DIM)
    ang_r = rows[:, None] * inv
    ang_c = cols[:, None] * inv
    ang = jnp.concatenate([ang_r, ang_r, ang_c, ang_c], axis=-1)
    cos, sin = jnp.cos(ang), jnp.sin(ang)
    one = jnp.ones((seq, 64), f32)
    lat = dict(k=jnp.concatenate([cos, cos, sin, sin], 1),
               q=jnp.concatenate([one, cos, sin, cos, sin, one], 1))
    ctx = dict(k=jnp.concatenate([jnp.ones((n_ctx, 64), f32), jnp.zeros((n_ctx, 64), f32)], 1))
    return lat, ctx


def kernel(x, c, ctx, c_ctx, w_ada, b_ada, norm1_w, ffn1_w1, ffn1_w3, ffn1_w2, norm2_w, w_in,
           q_a_norm_w, w_uq, kv_a_norm_w, w_ukv, q_norm_w, k_norm_w, v_norm_w, w_s, b_s, w_out,
           norm3_w, ffn2_w1, ffn2_w3, ffn2_w2):
    bsz, seq, _ = x.shape
    n_ctx = ctx.shape[1]
    rows = -(-(bsz + 1) // 8) * 8
    cc = jnp.concatenate([c, c_ctx[None], jnp.zeros((rows - bsz - 1, D_MODEL), jnp.float32)], 0)
    mod = _ada_call(cc, w_ada[0], b_ada[0][None]).reshape(rows, N_MOD, 1, D_MODEL)
    mod_ctx = mod[bsz:bsz + 1]

    f1 = _ffn_weights(ffn1_w1[0], ffn1_w3[0], ffn1_w2[0])
    f2 = _ffn_weights(ffn2_w1[0], ffn2_w3[0], ffn2_w2[0])
    wts = _mix_weights(norm2_w[0], w_in[0], q_a_norm_w[0], w_uq[0], kv_a_norm_w[0], w_ukv[0],
                       q_norm_w[0], k_norm_w[0], v_norm_w[0], w_s[0], b_s[0])
    tabs_lat, tabs_ctx = _rope_tables(seq, n_ctx)

    x1 = _ffn_call(x, mod, 0, True, norm1_w, *f1, tm=ROW_TILE)
    ctx1 = _ffn_call(ctx.reshape(1, bsz * n_ctx, D_MODEL), mod_ctx, 0, False, norm1_w, *f1,
                     tm=ROW_TILE).reshape(bsz, n_ctx, D_MODEL)
    k_lat, v_lat, q, sg = _prep_call(x1, mod, True, wts, tabs_lat, tm=PREP_TILE, with_q=True)
    k_ctx, v_ctx = _prep_call(ctx1, mod_ctx, False, wts, tabs_ctx, tm=n_ctx, with_q=False)
    attn = _attn_call(q, k_lat, k_ctx, v_lat, v_ctx, tq=Q_TILE)
    w_out_r = w_out[0].astype(jnp.bfloat16).reshape(2, MLA_HEADS * V_HEAD, D_MODEL)
    return _out_ffn_call(x1, attn, sg, mod, w_out_r, norm3_w, *f2, tm=ROW_TILE)
```

```python
import numpy as np
import jax
import jax.numpy as jnp
from jax import lax
from jax.experimental import pallas as pl
from jax.experimental.pallas import tpu as pltpu

D_MODEL = 1024
GRID_W = 64
MLA_HEADS = 8
QK_NOPE = 64
QK_ROPE = 32
QK_HEAD = QK_NOPE + QK_ROPE
V_HEAD = 64
Q_LORA = 256
KV_LORA = 128
AXIS_DIM = QK_ROPE // 2
ROPE_BASE = 10000.0
GMLP_GROUPS = 8
GMLP_GROUP_DIM = 64
GMLP_WIDTH = GMLP_GROUPS * GMLP_GROUP_DIM
CHUNK = 128
KV_COLS = KV_LORA + QK_ROPE
Q_START = KV_COLS
U_START = KV_COLS + Q_LORA
V_START = U_START + GMLP_WIDTH
IN_COLS = V_START + GMLP_WIDTH
D_FF = 2816
N_MOD = 9
EPS = 1e-6

LANES = 128
HEAD_PAD = LANES
N_PAIRS = MLA_HEADS // 2
VMEM_LIMIT = 56 * 1024 * 1024

EXT_KV = 0
EXT_AB = 128
EXT_Q = 256
EXT_U = EXT_Q + Q_LORA
EXT_V = EXT_U + GMLP_WIDTH
EXT_COLS = EXT_V + GMLP_WIDTH
EXT_KV_ONLY = EXT_Q

FF_TILE = 256
N_FF = D_FF // FF_TILE
ROW_TILE = 1024
PREP_TILE = 512
Q_TILE = 1024
Q_SUB = 512


def _rms_scale(x, n):
    return lax.rsqrt(jnp.sum(x * x, axis=-1, keepdims=True) * (1.0 / n) + EPS)


def _silu(a):
    return a / (1.0 + jnp.exp(-a))


def _gelu_tanh(x):
    c = np.float32(np.sqrt(2.0 / np.pi))
    t = jnp.tanh(x * (c + np.float32(c * 0.044715) * (x * x)))
    return x * (0.5 + 0.5 * t)


def _ada_kernel(c_ref, w_ref, b_ref, o_ref):
    s = _silu(c_ref[...]).astype(jnp.bfloat16)
    o_ref[...] = jnp.dot(s, w_ref[...].astype(jnp.bfloat16),
                         preferred_element_type=jnp.float32) + b_ref[...]


def _ada_call(cc, w_ada, b_ada):
    rows = cc.shape[0]
    n = w_ada.shape[1]
    tn = 1024
    return pl.pallas_call(
        _ada_kernel,
        grid=(n // tn,),
        in_specs=[pl.BlockSpec((rows, D_MODEL), lambda j: (0, 0)),
                  pl.BlockSpec((D_MODEL, tn), lambda j: (0, j)),
                  pl.BlockSpec((1, tn), lambda j: (0, j))],
        out_specs=pl.BlockSpec((rows, tn), lambda j: (0, j)),
        out_shape=jax.ShapeDtypeStruct((rows, n), jnp.float32),
        compiler_params=pltpu.CompilerParams(dimension_semantics=("arbitrary",),
                                             vmem_limit_bytes=VMEM_LIMIT),
        name="adaln",
    )(cc, w_ada, b_ada)


def _ffn_core(x, mod_ref, nw_ref, w1_ref, w3_ref, w2_ref, h_ref, acc_ref):
    shift, scale, gate = mod_ref[0, 0], mod_ref[0, 1], mod_ref[0, 2]
    h_ref[...] = (x * _rms_scale(x, D_MODEL) * (nw_ref[...] * (1.0 + scale)) + shift
                  ).astype(jnp.bfloat16)
    acc_ref[...] = jnp.zeros_like(acc_ref)

    for j in range(N_FF):
        hb = h_ref[...]
        cols = slice(j * FF_TILE, (j + 1) * FF_TILE)
        a = jnp.dot(hb, w1_ref[:, cols], preferred_element_type=jnp.float32)
        b = jnp.dot(hb, w3_ref[:, cols], preferred_element_type=jnp.float32)
        g = (_silu(a) * b).astype(jnp.bfloat16)
        acc_ref[...] += jnp.dot(g, w2_ref[j], preferred_element_type=jnp.float32)
    return x + (0.5 * gate) * acc_ref[...]


def _ffn_kernel(x_ref, mod_ref, nw_ref, w1_ref, w3_ref, w2_ref, o_ref, h_ref, acc_ref):
    o_ref[0] = _ffn_core(x_ref[0], mod_ref, nw_ref, w1_ref, w3_ref, w2_ref, h_ref, acc_ref)


def _const_spec(shape):
    nd = len(shape)
    return pl.BlockSpec(shape, lambda *_: (0,) * nd, pipeline_mode=pl.Buffered(1))


def _mod_spec(mod_block, per_batch):
    if per_batch:
        return pl.BlockSpec((1, 3, 1, D_MODEL), lambda b, i: (b, mod_block, 0, 0))
    return pl.BlockSpec((1, 3, 1, D_MODEL), lambda b, i: (0, mod_block, 0, 0))


def _ffn_call(x, mod, mod_block, per_batch, norm_w, w1, w3, w2, tm):
    bsz, seq, _ = x.shape
    assert seq % tm == 0, (seq, tm)
    return pl.pallas_call(
        _ffn_kernel,
        grid=(bsz, seq // tm),
        in_specs=[pl.BlockSpec((1, tm, D_MODEL), lambda b, i: (b, i, 0)),
                  _mod_spec(mod_block, per_batch),
                  _const_spec((1, D_MODEL)),
                  _const_spec((D_MODEL, D_FF)),
                  _const_spec((D_MODEL, D_FF)),
                  _const_spec((N_FF, FF_TILE, D_MODEL))],
        out_specs=pl.BlockSpec((1, tm, D_MODEL), lambda b, i: (b, i, 0)),
        out_shape=jax.ShapeDtypeStruct(x.shape, jnp.float32),
        scratch_shapes=[pltpu.VMEM((tm, D_MODEL), jnp.bfloat16),
                        pltpu.VMEM((tm, D_MODEL), jnp.float32)],
        compiler_params=pltpu.CompilerParams(dimension_semantics=("arbitrary", "arbitrary"),
                                             vmem_limit_bytes=VMEM_LIMIT),
        name="ffn",
    )(x, mod, norm_w, w1, w3, w2)


def _lane_iota(shape):
    return lax.broadcasted_iota(jnp.int32, shape, len(shape) - 1)


def _sq_bf16(x):
    return (x * x).astype(jnp.bfloat16)


def _seg_rms(sq, sel_ref, n):
    ss = jnp.dot(sq, sel_ref[...], preferred_element_type=jnp.float32)
    return lax.rsqrt(ss + n * EPS)


def _kv_prep(proj, kvn_ref, wukv_ref, wkn_ref, wkab_ref, tk_ref, selk_ref, k_ref, v_ref):
    tm = proj.shape[0]
    ckv = proj[:, EXT_KV:EXT_KV + KV_LORA]
    ckv = (ckv * _rms_scale(ckv, KV_LORA) * kvn_ref[...]).astype(jnp.bfloat16)
    kv = jnp.dot(ckv, wukv_ref[...], preferred_element_type=jnp.float32)
    lane = _lane_iota((tm, LANES))
    lo = lane < 64
    ab = proj[:, EXT_AB:EXT_AB + LANES]
    root_n = np.float32(np.sqrt(QK_HEAD))
    t = ab * ((wkab_ref[...] * root_n) * tk_ref[...])
    rope = t + pltpu.roll(t, 64, axis=1)
    ab_sq = _sq_bf16(ab)
    one_e = jnp.where(lane == V_HEAD, 1.0, 0.0)
    one_o = jnp.where(lane == 0, 1.0, 0.0)
    for p in range(N_PAIRS):
        vp = kv[:, MLA_HEADS * QK_NOPE + p * LANES:MLA_HEADS * QK_NOPE + (p + 1) * LANES]
        v_ref[0, :, (2 * p) * LANES:(2 * p + 1) * LANES] = jnp.where(lo, vp, one_e).astype(jnp.bfloat16)
        v_ref[0, :, (2 * p + 1) * LANES:(2 * p + 2) * LANES] = jnp.where(lo, one_o, vp).astype(jnp.bfloat16)
        kp = kv[:, p * LANES:(p + 1) * LANES]
        r = _seg_rms(jnp.concatenate([_sq_bf16(kp), ab_sq], axis=1), selk_ref, QK_HEAD)
        kw = kp * (wkn_ref[...] * root_n)
        k_ref[0, 2 * p] = (jnp.where(lo, kw, rope) * r[:, :LANES]).astype(jnp.bfloat16)
        k_ref[0, 2 * p + 1] = (jnp.where(lo, rope, kw) * r[:, LANES:]).astype(jnp.bfloat16)


def _modulated_proj(x, mod_ref, nw_ref, win_ref):
    shift, scale = mod_ref[0, 0], mod_ref[0, 1]
    h = (x * _rms_scale(x, D_MODEL) * (nw_ref[...] * (1.0 + scale)) + shift).astype(jnp.bfloat16)
    return jnp.dot(h, win_ref[...], preferred_element_type=jnp.float32)


def _prep_kernel(x_ref, mod_ref, nw_ref, win_ref, kvn_ref, wukv_ref, wkn_ref, wkab_ref, tk_ref,
                 selk_ref, qan_ref, wuq_ref, wqn_ref, tq_ref, selq_ref, vnw_ref, selv_ref,
                 ws_ref, bs_ref, k_ref, v_ref, q_ref, sg_ref):
    tm = x_ref.shape[1]
    proj = _modulated_proj(x_ref[0], mod_ref, nw_ref, win_ref)
    _kv_prep(proj, kvn_ref, wukv_ref, wkn_ref, wkab_ref, tk_ref, selk_ref, k_ref, v_ref)

    cq = proj[:, EXT_Q:EXT_Q + Q_LORA]
    cq = (cq * _rms_scale(cq, Q_LORA) * qan_ref[...]).astype(jnp.bfloat16)
    qall = jnp.dot(cq, wuq_ref[...], preferred_element_type=jnp.float32)
    lane = _lane_iota((tm, LANES))
    tabs = [wqn_ref[:, par * LANES:(par + 1) * LANES] * tq_ref[:, par * LANES:(par + 1) * LANES]
            * np.float32(np.log2(np.e)) for par in range(2)]
    for p in range(N_PAIRS):
        qp = qall[:, 2 * p * LANES:(2 * p + 2) * LANES]
        r = _seg_rms(_sq_bf16(qp), selq_ref, QK_HEAD)
        for par in range(2):
            hl = slice(par * LANES, (par + 1) * LANES)
            q_ref[0, 2 * p + par] = (qp[:, hl] * r[:, hl] * tabs[par]).astype(jnp.bfloat16)

    u = _gelu_tanh(proj[:, EXT_U:EXT_U + GMLP_WIDTH])
    v = _gelu_tanh(proj[:, EXT_V:EXT_V + GMLP_WIDTH])
    lo = lane < 64
    vn_tiles = []
    for p in range(GMLP_GROUPS // 2):
        if p % 2 == 0:
            v2 = v[:, p * LANES:(p + 2) * LANES]
            vn2 = (v2 * _seg_rms(_sq_bf16(v2), selv_ref, GMLP_GROUP_DIM)
                   * (vnw_ref[:, p * LANES:(p + 2) * LANES] * np.float32(np.sqrt(GMLP_GROUP_DIM))))
        vn = vn2[:, (p % 2) * LANES:(p % 2 + 1) * LANES]
        vn_tiles.append((jnp.where(lo, vn, 0.0).astype(jnp.bfloat16),
                         jnp.where(lo, 0.0, vn).astype(jnp.bfloat16)))
    for c in range(tm // CHUNK):
        rows = slice(c * CHUNK, (c + 1) * CHUNK)
        for p in range(GMLP_GROUPS // 2):
            vblk = jnp.concatenate([vn_tiles[p][0][rows], vn_tiles[p][1][rows]], axis=0)
            s = (jnp.dot(ws_ref[p], vblk, preferred_element_type=jnp.float32)
                 + bs_ref[:, p * LANES:(p + 1) * LANES])
            sg_ref[0, rows, p * LANES:(p + 1) * LANES] = (
                u[rows, p * LANES:(p + 1) * LANES] * s).astype(jnp.bfloat16)


def _kvonly_kernel(x_ref, mod_ref, nw_ref, win_ref, kvn_ref, wukv_ref, wkn_ref, wkab_ref, tk_ref,
                   selk_ref, k_ref, v_ref):
    proj = _modulated_proj(x_ref[0], mod_ref, nw_ref, win_ref)
    _kv_prep(proj, kvn_ref, wukv_ref, wkn_ref, wkab_ref, tk_ref, selk_ref, k_ref, v_ref)


def _prep_call(x, mod, per_batch, wts, tabs, tm, with_q):
    bsz, seq, _ = x.shape
    assert seq % tm == 0 and tm % CHUNK == 0, (seq, tm)
    x_spec = pl.BlockSpec((1, tm, D_MODEL), lambda b, i: (b, i, 0))
    tab_spec = lambda w: pl.BlockSpec((tm, w), lambda b, i: (i, 0))
    ncols = EXT_COLS if with_q else EXT_KV_ONLY
    kv_specs = [x_spec, _mod_spec(1, per_batch), _const_spec((1, D_MODEL)),
                _const_spec((D_MODEL, ncols)), _const_spec((1, KV_LORA)),
                _const_spec((KV_LORA, MLA_HEADS * LANES)), _const_spec((1, LANES)),
                _const_spec((1, LANES)), tab_spec(LANES), _const_spec((2 * LANES, 2 * LANES))]
    kv_args = [x, mod, wts["norm2"], wts["w_in"] if with_q else wts["w_in"][:, :EXT_KV_ONLY],
               wts["kvn"], wts["w_ukv"], wts["wk_nope"], wts["wk_ab"], tabs["k"], wts["sel_k"]]
    k_shape = jax.ShapeDtypeStruct((bsz, MLA_HEADS, seq, HEAD_PAD), jnp.bfloat16)
    v_shape = jax.ShapeDtypeStruct((bsz, seq, MLA_HEADS * LANES), jnp.bfloat16)
    sg_shape = jax.ShapeDtypeStruct((bsz, seq, GMLP_WIDTH), jnp.bfloat16)
    k_spec = pl.BlockSpec((1, MLA_HEADS, tm, HEAD_PAD), lambda b, i: (b, 0, i, 0))
    v_spec = pl.BlockSpec((1, tm, MLA_HEADS * LANES), lambda b, i: (b, i, 0))
    sg_spec = pl.BlockSpec((1, tm, GMLP_WIDTH), lambda b, i: (b, i, 0))
    params = pltpu.CompilerParams(dimension_semantics=("arbitrary", "arbitrary"),
                                  vmem_limit_bytes=VMEM_LIMIT)
    if not with_q:
        return pl.pallas_call(
            _kvonly_kernel, grid=(bsz, seq // tm), in_specs=kv_specs,
            out_specs=[k_spec, v_spec], out_shape=[k_shape, v_shape],
            compiler_params=params, name="kv_prep")(*kv_args)
    q_specs = [_const_spec((1, Q_LORA)), _const_spec((Q_LORA, MLA_HEADS * LANES)),
               _const_spec((1, 2 * LANES)), tab_spec(2 * LANES), _const_spec((2 * LANES, 2 * LANES)),
               _const_spec((1, GMLP_WIDTH)), _const_spec((2 * LANES, 2 * LANES)),
               _const_spec((GMLP_GROUPS // 2, CHUNK, 2 * CHUNK)), _const_spec((CHUNK, GMLP_WIDTH))]
    q_args = [wts["qan"], wts["w_uq"], wts["wq"], tabs["q"], wts["sel_q"], wts["vnw"], wts["sel_v"],
              wts["ws"], wts["bs"]]
    return pl.pallas_call(
        _prep_kernel, grid=(bsz, seq // tm), in_specs=kv_specs + q_specs,
        out_specs=[k_spec, v_spec, k_spec, sg_spec],
        out_shape=[k_shape, v_shape, k_shape, sg_shape],
        compiler_params=params, name="mix_prep")(*kv_args, *q_args)


def _softmax_probs(q, kl, kc):
    nt = (((1,), (1,)), ((), ()))
    s1 = lax.dot_general(q, kl, nt, preferred_element_type=jnp.float32)
    s2 = lax.dot_general(q, kc, nt, preferred_element_type=jnp.float32)
    m = jnp.maximum(jnp.max(s1, axis=-1, keepdims=True), jnp.max(s2, axis=-1, keepdims=True))
    return jnp.exp2(s1 - m).astype(jnp.bfloat16), jnp.exp2(s2 - m).astype(jnp.bfloat16)


def _attn_kernel(q_ref, kl_ref, kc_ref, vl_ref, vc_ref, qn_ref, kln_ref, kcn_ref, o_ref,
                 p1_ref, p2_ref):
    @pl.when(pl.program_id(0) == 0)
    def _():
        p1, p2 = _softmax_probs(q_ref[0, 0, :Q_SUB], kl_ref[0, 0], kc_ref[0, 0])
        p1_ref[...] = p1
        p2_ref[...] = p2

    for r0 in range(0, q_ref.shape[2], Q_SUB):
        _attn_rows(slice(r0, r0 + Q_SUB), r0 == 0, q_ref, kl_ref, kc_ref, vl_ref, vc_ref, o_ref,
                   p1_ref, p2_ref)
    p1, p2 = _softmax_probs(qn_ref[0, 0], kln_ref[0, 0], kcn_ref[0, 0])
    p1_ref[...] = p1
    p2_ref[...] = p2


def _attn_rows(rows, carried_first, q_ref, kl_ref, kc_ref, vl_ref, vc_ref, o_ref, p1_ref, p2_ref):
    lane = _lane_iota((Q_SUB, LANES))
    lo = lane < 64
    ones_lane = (lane == V_HEAD, lane == 0)
    outs = []
    for h in range(MLA_HEADS):
        par = h % 2
        pv = slice((h - par) * LANES, (h - par + 2) * LANES)
        if carried_first and h == 0:
            p1, p2 = p1_ref[...], p2_ref[...]
        else:
            p1, p2 = _softmax_probs(q_ref[0, h, rows], kl_ref[0, h], kc_ref[0, h])
        o = (jnp.dot(p1, vl_ref[0, :, pv], preferred_element_type=jnp.float32)
             + jnp.dot(p2, vc_ref[0, :, pv], preferred_element_type=jnp.float32))
        o = o[:, par * LANES:(par + 1) * LANES]
        l = jnp.sum(jnp.where(ones_lane[par], o, 0.0), axis=-1, keepdims=True)
        outs.append(o / l)
    for p in range(N_PAIRS):
        o_ref[0, rows, p * LANES:(p + 1) * LANES] = jnp.where(
            lo, outs[2 * p], outs[2 * p + 1]).astype(jnp.bfloat16)


def _attn_call(q, k_lat, k_ctx, v_lat, v_ctx, tq):
    bsz, _, seq, _ = q.shape
    assert seq % tq == 0 and tq % Q_SUB == 0, (seq, tq)
    n_ctx = k_ctx.shape[2]
    per_b = seq // tq
    n_steps = bsz * per_b
    sub_per_tile = tq // Q_SUB

    def nxt(t):
        return jnp.minimum(t + 1, n_steps - 1)

    return pl.pallas_call(
        _attn_kernel,
        grid=(n_steps,),
        in_specs=[pl.BlockSpec((1, MLA_HEADS, tq, HEAD_PAD), lambda t: (t // per_b, 0, t % per_b, 0)),
                  pl.BlockSpec((1, MLA_HEADS, seq, HEAD_PAD), lambda t: (t // per_b, 0, 0, 0)),
                  pl.BlockSpec((1, MLA_HEADS, n_ctx, HEAD_PAD), lambda t: (t // per_b, 0, 0, 0)),
                  pl.BlockSpec((1, seq, MLA_HEADS * LANES), lambda t: (t // per_b, 0, 0)),
                  pl.BlockSpec((1, n_ctx, MLA_HEADS * LANES), lambda t: (t // per_b, 0, 0)),
                  pl.BlockSpec((1, 1, Q_SUB, HEAD_PAD),
                               lambda t: (nxt(t) // per_b, 0, (nxt(t) % per_b) * sub_per_tile, 0)),
                  pl.BlockSpec((1, 1, seq, HEAD_PAD), lambda t: (nxt(t) // per_b, 0, 0, 0)),
                  pl.BlockSpec((1, 1, n_ctx, HEAD_PAD), lambda t: (nxt(t) // per_b, 0, 0, 0))],
        out_specs=pl.BlockSpec((1, tq, MLA_HEADS * V_HEAD), lambda t: (t // per_b, t % per_b, 0)),
        out_shape=jax.ShapeDtypeStruct((bsz, seq, MLA_HEADS * V_HEAD), jnp.bfloat16),
        scratch_shapes=[pltpu.VMEM((Q_SUB, seq), jnp.bfloat16),
                        pltpu.VMEM((Q_SUB, n_ctx), jnp.bfloat16)],
        compiler_params=pltpu.CompilerParams(
            dimension_semantics=("arbitrary",),
            vmem_limit_bytes=VMEM_LIMIT),
        name="attention",
    )(q, k_lat, k_ctx, v_lat, v_ctx, q, k_lat, k_ctx)


def _out_ffn_kernel(x_ref, attn_ref, sg_ref, modm_ref, wout_ref, mod_ref, nw_ref,
                    w1_ref, w3_ref, w2_ref, o_ref, h_ref, acc_ref):
    y = (jnp.dot(attn_ref[0], wout_ref[0], preferred_element_type=jnp.float32)
         + jnp.dot(sg_ref[0], wout_ref[1], preferred_element_type=jnp.float32))
    x = x_ref[0] + modm_ref[0, 2] * y
    o_ref[0] = _ffn_core(x, mod_ref, nw_ref, w1_ref, w3_ref, w2_ref, h_ref, acc_ref)


def _out_ffn_call(x, attn, sg, mod, w_out, norm_w, w1, w3, w2, tm):
    bsz, seq, _ = x.shape
    assert seq % tm == 0, (seq, tm)
    half = MLA_HEADS * V_HEAD
    row = lambda w: pl.BlockSpec((1, tm, w), lambda b, i: (b, i, 0))
    return pl.pallas_call(
        _out_ffn_kernel,
        grid=(bsz, seq // tm),
        in_specs=[row(D_MODEL), row(half), row(GMLP_WIDTH), _mod_spec(1, True),
                  _const_spec((2, half, D_MODEL)), _mod_spec(2, True), _const_spec((1, D_MODEL)),
                  _const_spec((D_MODEL, D_FF)), _const_spec((D_MODEL, D_FF)),
                  _const_spec((N_FF, FF_TILE, D_MODEL))],
        out_specs=row(D_MODEL),
        out_shape=jax.ShapeDtypeStruct(x.shape, jnp.float32),
        scratch_shapes=[pltpu.VMEM((tm, D_MODEL), jnp.bfloat16),
                        pltpu.VMEM((tm, D_MODEL), jnp.float32)],
        compiler_params=pltpu.CompilerParams(dimension_semantics=("arbitrary", "arbitrary"),
                                             vmem_limit_bytes=VMEM_LIMIT),
        name="out_ffn",
    )(x, attn, sg, mod, w_out, mod, norm_w, w1, w3, w2)


def _ffn_weights(w1, w3, w2):
    bf = jnp.bfloat16
    return w1.astype(bf), w3.astype(bf), w2.astype(bf).reshape(N_FF, FF_TILE, D_MODEL)


def _rot_cols(w, start, signed=True):
    half = AXIS_DIM // 2
    parts = []
    for blk in range(QK_ROPE // half):
        src = start + (blk + 1) * half if blk % 2 == 0 else start + (blk - 1) * half
        piece = w[..., src:src + half]
        parts.append(-piece if (signed and blk % 2 == 0) else piece)
    return jnp.concatenate(parts, axis=-1)


def _selectors():
    r = np.arange(2 * LANES)[:, None]
    c = np.arange(2 * LANES)[None, :]
    sel_k = ((r < 64) & (c < LANES)) | ((r >= 64) & (r < LANES) & (c >= LANES)) | (
        (r >= LANES) & (r < LANES + QK_ROPE))
    ro = r - LANES
    sel_q = ((r < QK_HEAD) & (c < LANES)) | (
        (r >= LANES) & ((ro < QK_ROPE) | (ro >= 2 * QK_ROPE)) & (c >= LANES))
    sel_v = (r // GMLP_GROUP_DIM) == (c // GMLP_GROUP_DIM)
    return [jnp.asarray(m, jnp.bfloat16) for m in (sel_k, sel_q, sel_v)]


def _mix_weights(norm2_w, w_in, q_a_norm_w, w_uq, kv_a_norm_w, w_ukv, q_norm_w, k_norm_w,
                 v_norm_w, w_s, b_s):
    bf = jnp.bfloat16
    kpe = w_in[:, KV_LORA:KV_COLS]
    kpe_rot = _rot_cols(w_in, KV_LORA)
    w_in_ext = jnp.concatenate([w_in[:, :KV_LORA], kpe, kpe, kpe_rot, kpe_rot,
                                w_in[:, Q_START:]], axis=1).astype(bf)
    ukv = w_ukv.reshape(KV_LORA, MLA_HEADS, QK_NOPE + V_HEAD)
    w_ukv_p = jnp.concatenate([ukv[:, :, :QK_NOPE].reshape(KV_LORA, -1),
                               ukv[:, :, QK_NOPE:].reshape(KV_LORA, -1)], axis=1).astype(bf)
    cols = []
    for h in range(MLA_HEADS):
        base = h * QK_HEAD
        nope = w_uq[:, base:base + QK_NOPE]
        pes = w_uq[:, base + QK_NOPE:base + QK_HEAD]
        rots = _rot_cols(w_uq, base + QK_NOPE)
        cols += [nope, pes, rots] if h % 2 == 0 else [pes, rots, nope]
    w_uq_ext = jnp.concatenate(cols, axis=1).astype(bf)
    qn, qp, qr = q_norm_w[:QK_NOPE], q_norm_w[QK_NOPE:], _rot_cols(q_norm_w, QK_NOPE, signed=False)
    wq = jnp.concatenate([qn, qp, qr, qp, qr, qn])[None]
    kn, kp, kr = k_norm_w[:QK_NOPE], k_norm_w[QK_NOPE:], _rot_cols(k_norm_w, QK_NOPE, signed=False)
    sel_k, sel_q, sel_v = _selectors()
    return dict(
        sel_k=sel_k, sel_q=sel_q, sel_v=sel_v,
        norm2=norm2_w[None], w_in=w_in_ext, kvn=kv_a_norm_w[None], w_ukv=w_ukv_p,
        wk_nope=jnp.concatenate([kn, kn])[None],
        wk_ab=jnp.concatenate([kp, kp, kr, kr])[None],
        qan=q_a_norm_w[None], w_uq=w_uq_ext, wq=wq,
        vnw=v_norm_w.reshape(1, GMLP_WIDTH),
        ws=w_s.astype(bf).reshape(GMLP_GROUPS // 2, 2, CHUNK, CHUNK).transpose(0, 2, 1, 3
                                   ).reshape(GMLP_GROUPS // 2, CHUNK, 2 * CHUNK),
        bs=jnp.broadcast_to(b_s.T[:, :, None], (CHUNK, GMLP_GROUPS, GMLP_GROUP_DIM)
                            ).reshape(CHUNK, GMLP_WIDTH),
    )


def _rope_tables(seq, n_ctx):
    f32 = jnp.float32
    rows_n = seq // GRID_W
    rows = jnp.repeat(jnp.arange(rows_n, dtype=f32), GRID_W)
    cols = jnp.tile(jnp.arange(GRID_W, dtype=f32), rows_n)
    inv = ROPE_BASE ** (-jnp.arange(0, AXIS_DIM, 2, dtype=f32) / AXIS_DIM)
    ang_r = rows[:, None] * inv
    ang_c = cols[:, None] * inv
    ang = jnp.concatenate([ang_r, ang_r, ang_c, ang_c], axis=-1)
    cos, sin = jnp.cos(ang), jnp.sin(ang)
    one = jnp.ones((seq, 64), f32)
    lat = dict(k=jnp.concatenate([cos, cos, sin, sin], 1),
               q=jnp.concatenate([one, cos, sin, cos, sin, one], 1))
    ctx = dict(k=jnp.concatenate([jnp.ones((n_ctx, 64), f32), jnp.zeros((n_ctx, 64), f32)], 1))
    return lat, ctx


def kernel(x, c, ctx, c_ctx, w_ada, b_ada, norm1_w, ffn1_w1, ffn1_w3, ffn1_w2, norm2_w, w_in,
           q_a_norm_w, w_uq, kv_a_norm_w, w_ukv, q_norm_w, k_norm_w, v_norm_w, w_s, b_s, w_out,
           norm3_w, ffn2_w1, ffn2_w3, ffn2_w2):
    bsz, seq, _ = x.shape
    n_ctx = ctx.shape[1]
    rows = -(-(bsz + 1) // 8) * 8
    cc = jnp.concatenate([c, c_ctx[None], jnp.zeros((rows - bsz - 1, D_MODEL), jnp.float32)], 0)
    mod = _ada_call(cc, w_ada[0], b_ada[0][None]).reshape(rows, N_MOD, 1, D_MODEL)
    mod_ctx = mod[bsz:bsz + 1]

    f1 = _ffn_weights(ffn1_w1[0], ffn1_w3[0], ffn1_w2[0])
    f2 = _ffn_weights(ffn2_w1[0], ffn2_w3[0], ffn2_w2[0])
    wts = _mix_weights(norm2_w[0], w_in[0], q_a_norm_w[0], w_uq[0], kv_a_norm_w[0], w_ukv[0],
                       q_norm_w[0], k_norm_w[0], v_norm_w[0], w_s[0], b_s[0])
    tabs_lat, tabs_ctx = _rope_tables(seq, n_ctx)

    x1 = _ffn_call(x, mod, 0, True, norm1_w, *f1, tm=ROW_TILE)
    ctx1 = _ffn_call(ctx.reshape(1, bsz * n_ctx, D_MODEL), mod_ctx, 0, False, norm1_w, *f1,
                     tm=ROW_TILE).reshape(bsz, n_ctx, D_MODEL)
    k_lat, v_lat, q, sg = _prep_call(x1, mod, True, wts, tabs_lat, tm=PREP_TILE, with_q=True)
    k_ctx, v_ctx = _prep_call(ctx1, mod_ctx, False, wts, tabs_ctx, tm=n_ctx, with_q=False)
    attn = _attn_call(q, k_lat, k_ctx, v_lat, v_ctx, tq=Q_TILE)
    w_out_r = w_out[0].astype(jnp.bfloat16).reshape(2, MLA_HEADS * V_HEAD, D_MODEL)
    return _out_ffn_call(x1, attn, sg, mod, w_out_r, norm3_w, *f2, tm=ROW_TILE)
```

```python
import numpy as np
import jax
import jax.numpy as jnp
from jax import lax
from jax.experimental import pallas as pl
from jax.experimental.pallas import tpu as pltpu

D_MODEL = 1024
GRID_W = 64
MLA_HEADS = 8
QK_NOPE = 64
QK_ROPE = 32
QK_HEAD = QK_NOPE + QK_ROPE
V_HEAD = 64
Q_LORA = 256
KV_LORA = 128
AXIS_DIM = QK_ROPE // 2
ROPE_BASE = 10000.0
GMLP_GROUPS = 8
GMLP_GROUP_DIM = 64
GMLP_WIDTH = GMLP_GROUPS * GMLP_GROUP_DIM
CHUNK = 128
KV_COLS = KV_LORA + QK_ROPE
Q_START = KV_COLS
U_START = KV_COLS + Q_LORA
V_START = U_START + GMLP_WIDTH
IN_COLS = V_START + GMLP_WIDTH
D_FF = 2816
N_MOD = 9
EPS = 1e-6

F8 = jnp.float8_e4m3fn
F8_MAX = 448.0
P_SHIFT = 7.0
LANES = 128
HEAD_PAD = LANES
N_PAIRS = MLA_HEADS // 2
VMEM_LIMIT = 56 * 1024 * 1024

EXT_KV = 0
EXT_AB = 128
EXT_Q = 256
EXT_U = EXT_Q + Q_LORA
EXT_V = EXT_U + GMLP_WIDTH
EXT_COLS = EXT_V + GMLP_WIDTH
EXT_KV_ONLY = EXT_Q

FF_TILE = 256
N_FF = D_FF // FF_TILE
ROW_TILE = 1024
PREP_TILE = 512
Q_TILE = 1024
Q_SUB = 512


def _rms_scale(x, n):
    return lax.rsqrt(jnp.sum(x * x, axis=-1, keepdims=True) * (1.0 / n) + EPS)


def _silu(a):
    return a / (1.0 + jnp.exp(-a))


def _gelu_tanh(x):
    c = np.float32(np.sqrt(2.0 / np.pi))
    t = jnp.tanh(x * (c + np.float32(c * 0.044715) * (x * x)))
    return x * (0.5 + 0.5 * t)


def _ada_kernel(c_ref, w_ref, b_ref, o_ref):
    s = _silu(c_ref[...]).astype(jnp.bfloat16)
    o_ref[...] = jnp.dot(s, w_ref[...].astype(jnp.bfloat16),
                         preferred_element_type=jnp.float32) + b_ref[...]


def _ada_call(cc, w_ada, b_ada):
    rows = cc.shape[0]
    n = w_ada.shape[1]
    tn = 1024
    return pl.pallas_call(
        _ada_kernel,
        grid=(n // tn,),
        in_specs=[pl.BlockSpec((rows, D_MODEL), lambda j: (0, 0)),
                  pl.BlockSpec((D_MODEL, tn), lambda j: (0, j)),
                  pl.BlockSpec((1, tn), lambda j: (0, j))],
        out_specs=pl.BlockSpec((rows, tn), lambda j: (0, j)),
        out_shape=jax.ShapeDtypeStruct((rows, n), jnp.float32),
        compiler_params=pltpu.CompilerParams(dimension_semantics=("arbitrary",),
                                             vmem_limit_bytes=VMEM_LIMIT),
        name="adaln",
    )(cc, w_ada, b_ada)


def _ffn_core(x, mod_ref, nw_ref, w1_ref, w3_ref, w2_ref, h_ref, acc_ref):
    shift, scale, gate = mod_ref[0, 0], mod_ref[0, 1], mod_ref[0, 2]
    h_ref[...] = (x * _rms_scale(x, D_MODEL) * (nw_ref[...] * (1.0 + scale)) + shift
                  ).astype(jnp.bfloat16)
    acc_ref[...] = jnp.zeros_like(acc_ref)

    for j in range(N_FF):
        hb = h_ref[...]
        cols = slice(j * FF_TILE, (j + 1) * FF_TILE)
        a = jnp.dot(hb, w1_ref[:, cols], preferred_element_type=jnp.float32)
        b = jnp.dot(hb, w3_ref[:, cols], preferred_element_type=jnp.float32)
        g = (_silu(a) * b).astype(jnp.bfloat16)
        acc_ref[...] += jnp.dot(g, w2_ref[j], preferred_element_type=jnp.float32)
    return x + (0.5 * gate) * acc_ref[...]


def _ffn_kernel(x_ref, mod_ref, nw_ref, w1_ref, w3_ref, w2_ref, o_ref, h_ref, acc_ref):
    o_ref[0] = _ffn_core(x_ref[0], mod_ref, nw_ref, w1_ref, w3_ref, w2_ref, h_ref, acc_ref)


def _const_spec(shape):
    nd = len(shape)
    return pl.BlockSpec(shape, lambda *_: (0,) * nd, pipeline_mode=pl.Buffered(1))


def _mod_spec(mod_block, per_batch):
    if per_batch:
        return pl.BlockSpec((1, 3, 1, D_MODEL), lambda b, i: (b, mod_block, 0, 0))
    return pl.BlockSpec((1, 3, 1, D_MODEL), lambda b, i: (0, mod_block, 0, 0))


def _ffn_call(x, mod, mod_block, per_batch, norm_w, w1, w3, w2, tm):
    bsz, seq, _ = x.shape
    assert seq % tm == 0, (seq, tm)
    return pl.pallas_call(
        _ffn_kernel,
        grid=(bsz, seq // tm),
        in_specs=[pl.BlockSpec((1, tm, D_MODEL), lambda b, i: (b, i, 0)),
                  _mod_spec(mod_block, per_batch),
                  _const_spec((1, D_MODEL)),
                  _const_spec((D_MODEL, D_FF)),
                  _const_spec((D_MODEL, D_FF)),
                  _const_spec((N_FF, FF_TILE, D_MODEL))],
        out_specs=pl.BlockSpec((1, tm, D_MODEL), lambda b, i: (b, i, 0)),
        out_shape=jax.ShapeDtypeStruct(x.shape, jnp.float32),
        scratch_shapes=[pltpu.VMEM((tm, D_MODEL), jnp.bfloat16),
                        pltpu.VMEM((tm, D_MODEL), jnp.float32)],
        compiler_params=pltpu.CompilerParams(dimension_semantics=("arbitrary", "arbitrary"),
                                             vmem_limit_bytes=VMEM_LIMIT),
        name="ffn",
    )(x, mod, norm_w, w1, w3, w2)


def _lane_iota(shape):
    return lax.broadcasted_iota(jnp.int32, shape, len(shape) - 1)


def _sq_bf16(x):
    return (x * x).astype(jnp.bfloat16)


def _seg_rms(sq, sel_ref, n):
    ss = jnp.dot(sq, sel_ref[...], preferred_element_type=jnp.float32)
    return lax.rsqrt(ss + n * EPS)


def _kv_prep(proj, kvn_ref, wukv_ref, wkn_ref, wkab_ref, tk_ref, selk_ref, k_ref, v_ref):
    tm = proj.shape[0]
    ckv = proj[:, EXT_KV:EXT_KV + KV_LORA]
    ckv = (ckv * _rms_scale(ckv, KV_LORA) * kvn_ref[...]).astype(jnp.bfloat16)
    kv = jnp.dot(ckv, wukv_ref[...], preferred_element_type=jnp.float32)
    lane = _lane_iota((tm, LANES))
    lo = lane < 64
    ab = proj[:, EXT_AB:EXT_AB + LANES]
    root_n = np.float32(np.sqrt(QK_HEAD))
    g_rope = jnp.clip(wkab_ref[...] * root_n, -0.5 * F8_MAX, 0.5 * F8_MAX)
    g_nope = jnp.clip(wkn_ref[...] * root_n, -F8_MAX, F8_MAX)
    t = ab * (g_rope * tk_ref[...])
    rope = t + pltpu.roll(t, 64, axis=1)
    ab_sq = _sq_bf16(ab)
    one_e = jnp.where(lane == V_HEAD, 1.0, 0.0)
    one_o = jnp.where(lane == 0, 1.0, 0.0)
    for p in range(N_PAIRS):
        vp = kv[:, MLA_HEADS * QK_NOPE + p * LANES:MLA_HEADS * QK_NOPE + (p + 1) * LANES]
        vp = jnp.clip(vp, -F8_MAX, F8_MAX)
        v_ref[0, :, (2 * p) * LANES:(2 * p + 1) * LANES] = jnp.where(lo, vp, one_e).astype(F8)
        v_ref[0, :, (2 * p + 1) * LANES:(2 * p + 2) * LANES] = jnp.where(lo, one_o, vp).astype(F8)
        kp = kv[:, p * LANES:(p + 1) * LANES]
        r = _seg_rms(jnp.concatenate([_sq_bf16(kp), ab_sq], axis=1), selk_ref, QK_HEAD)
        kw = kp * g_nope
        k_ref[0, 2 * p] = (jnp.where(lo, kw, rope) * r[:, :LANES]).astype(F8)
        k_ref[0, 2 * p + 1] = (jnp.where(lo, rope, kw) * r[:, LANES:]).astype(F8)


def _modulated_proj(x, mod_ref, nw_ref, win_ref):
    shift, scale = mod_ref[0, 0], mod_ref[0, 1]
    h = (x * _rms_scale(x, D_MODEL) * (nw_ref[...] * (1.0 + scale)) + shift).astype(jnp.bfloat16)
    return jnp.dot(h, win_ref[...], preferred_element_type=jnp.float32)


def _prep_kernel(x_ref, mod_ref, nw_ref, win_ref, kvn_ref, wukv_ref, wkn_ref, wkab_ref, tk_ref,
                 selk_ref, qan_ref, wuq_ref, wqn_ref, tq_ref, selq_ref, vnw_ref, selv_ref,
                 ws_ref, bs_ref, k_ref, v_ref, q_ref, sg_ref):
    tm = x_ref.shape[1]
    proj = _modulated_proj(x_ref[0], mod_ref, nw_ref, win_ref)
    _kv_prep(proj, kvn_ref, wukv_ref, wkn_ref, wkab_ref, tk_ref, selk_ref, k_ref, v_ref)

    cq = proj[:, EXT_Q:EXT_Q + Q_LORA]
    cq = (cq * _rms_scale(cq, Q_LORA) * qan_ref[...]).astype(jnp.bfloat16)
    qall = jnp.dot(cq, wuq_ref[...], preferred_element_type=jnp.float32)
    lane = _lane_iota((tm, LANES))
    gq = jnp.clip(wqn_ref[...] * np.float32(np.log2(np.e)), -F8_MAX, F8_MAX)
    tabs = [gq[:, par * LANES:(par + 1) * LANES] * tq_ref[:, par * LANES:(par + 1) * LANES]
            for par in range(2)]
    for p in range(N_PAIRS):
        qp = qall[:, 2 * p * LANES:(2 * p + 2) * LANES]
        r = _seg_rms(_sq_bf16(qp), selq_ref, QK_HEAD)
        for par in range(2):
            hl = slice(par * LANES, (par + 1) * LANES)
            q_ref[0, 2 * p + par] = (qp[:, hl] * r[:, hl] * tabs[par]).astype(F8)

    u = _gelu_tanh(proj[:, EXT_U:EXT_U + GMLP_WIDTH])
    v = _gelu_tanh(proj[:, EXT_V:EXT_V + GMLP_WIDTH])
    lo = lane < 64
    vn_tiles = []
    for p in range(GMLP_GROUPS // 2):
        if p % 2 == 0:
            v2 = v[:, p * LANES:(p + 2) * LANES]
            vn2 = (v2 * _seg_rms(_sq_bf16(v2), selv_ref, GMLP_GROUP_DIM)
                   * (vnw_ref[:, p * LANES:(p + 2) * LANES] * np.float32(np.sqrt(GMLP_GROUP_DIM))))
        vn = vn2[:, (p % 2) * LANES:(p % 2 + 1) * LANES]
        vn_tiles.append((jnp.where(lo, vn, 0.0).astype(jnp.bfloat16),
                         jnp.where(lo, 0.0, vn).astype(jnp.bfloat16)))
    for c in range(tm // CHUNK):
        rows = slice(c * CHUNK, (c + 1) * CHUNK)
        for p in range(GMLP_GROUPS // 2):
            vblk = jnp.concatenate([vn_tiles[p][0][rows], vn_tiles[p][1][rows]], axis=0)
            s = (jnp.dot(ws_ref[p], vblk, preferred_element_type=jnp.float32)
                 + bs_ref[:, p * LANES:(p + 1) * LANES])
            sg_ref[0, rows, p * LANES:(p + 1) * LANES] = (
                u[rows, p * LANES:(p + 1) * LANES] * s).astype(jnp.bfloat16)


def _kvonly_kernel(x_ref, mod_ref, nw_ref, win_ref, kvn_ref, wukv_ref, wkn_ref, wkab_ref, tk_ref,
                   selk_ref, k_ref, v_ref):
    proj = _modulated_proj(x_ref[0], mod_ref, nw_ref, win_ref)
    _kv_prep(proj, kvn_ref, wukv_ref, wkn_ref, wkab_ref, tk_ref, selk_ref, k_ref, v_ref)


def _prep_call(x, mod, per_batch, wts, tabs, tm, with_q):
    bsz, seq, _ = x.shape
    assert seq % tm == 0 and tm % CHUNK == 0, (seq, tm)
    x_spec = pl.BlockSpec((1, tm, D_MODEL), lambda b, i: (b, i, 0))
    tab_spec = lambda w: pl.BlockSpec((tm, w), lambda b, i: (i, 0))
    ncols = EXT_COLS if with_q else EXT_KV_ONLY
    kv_specs = [x_spec, _mod_spec(1, per_batch), _const_spec((1, D_MODEL)),
                _const_spec((D_MODEL, ncols)), _const_spec((1, KV_LORA)),
                _const_spec((KV_LORA, MLA_HEADS * LANES)), _const_spec((1, LANES)),
                _const_spec((1, LANES)), tab_spec(LANES), _const_spec((2 * LANES, 2 * LANES))]
    kv_args = [x, mod, wts["norm2"], wts["w_in"] if with_q else wts["w_in"][:, :EXT_KV_ONLY],
               wts["kvn"], wts["w_ukv"], wts["wk_nope"], wts["wk_ab"], tabs["k"], wts["sel_k"]]
    k_shape = jax.ShapeDtypeStruct((bsz, MLA_HEADS, seq, HEAD_PAD), F8)
    v_shape = jax.ShapeDtypeStruct((bsz, seq, MLA_HEADS * LANES), F8)
    sg_shape = jax.ShapeDtypeStruct((bsz, seq, GMLP_WIDTH), jnp.bfloat16)
    k_spec = pl.BlockSpec((1, MLA_HEADS, tm, HEAD_PAD), lambda b, i: (b, 0, i, 0))
    v_spec = pl.BlockSpec((1, tm, MLA_HEADS * LANES), lambda b, i: (b, i, 0))
    sg_spec = pl.BlockSpec((1, tm, GMLP_WIDTH), lambda b, i: (b, i, 0))
    params = pltpu.CompilerParams(dimension_semantics=("arbitrary", "arbitrary"),
                                  vmem_limit_bytes=VMEM_LIMIT)
    if not with_q:
        return pl.pallas_call(
            _kvonly_kernel, grid=(bsz, seq // tm), in_specs=kv_specs,
            out_specs=[k_spec, v_spec], out_shape=[k_shape, v_shape],
            compiler_params=params, name="kv_prep")(*kv_args)
    q_specs = [_const_spec((1, Q_LORA)), _const_spec((Q_LORA, MLA_HEADS * LANES)),
               _const_spec((1, 2 * LANES)), tab_spec(2 * LANES), _const_spec((2 * LANES, 2 * LANES)),
               _const_spec((1, GMLP_WIDTH)), _const_spec((2 * LANES, 2 * LANES)),
               _const_spec((GMLP_GROUPS // 2, CHUNK, 2 * CHUNK)), _const_spec((CHUNK, GMLP_WIDTH))]
    q_args = [wts["qan"], wts["w_uq"], wts["wq"], tabs["q"], wts["sel_q"], wts["vnw"], wts["sel_v"],
              wts["ws"], wts["bs"]]
    return pl.pallas_call(
        _prep_kernel, grid=(bsz, seq // tm), in_specs=kv_specs + q_specs,
        out_specs=[k_spec, v_spec, k_spec, sg_spec],
        out_shape=[k_shape, v_shape, k_shape, sg_shape],
        compiler_params=params, name="mix_prep")(*kv_args, *q_args)


def _attn_kernel(q_ref, kl_ref, kc_ref, vl_ref, vc_ref, o_ref):
    for r0 in range(0, q_ref.shape[2], Q_SUB):
        _attn_rows(slice(r0, r0 + Q_SUB), q_ref, kl_ref, kc_ref, vl_ref, vc_ref, o_ref)


def _attn_rows(rows, q_ref, kl_ref, kc_ref, vl_ref, vc_ref, o_ref):
    nt = (((1,), (1,)), ((), ()))
    lane = _lane_iota((Q_SUB, LANES))
    lo = lane < 64
    ones_lane = (lane == V_HEAD, lane == 0)
    outs = []
    for h in range(MLA_HEADS):
        q = q_ref[0, h, rows]
        par = h % 2
        pv = slice((h - par) * LANES, (h - par + 2) * LANES)
        s1 = lax.dot_general(q, kl_ref[0, h], nt, preferred_element_type=jnp.float32)
        s2 = lax.dot_general(q, kc_ref[0, h], nt, preferred_element_type=jnp.float32)
        m = jnp.maximum(jnp.max(s1, axis=-1, keepdims=True),
                        jnp.max(s2, axis=-1, keepdims=True)) - P_SHIFT
        p1 = jnp.exp2(s1 - m).astype(F8)
        p2 = jnp.exp2(s2 - m).astype(F8)
        o = (jnp.dot(p1, vl_ref[0, :, pv], preferred_element_type=jnp.float32)
             + jnp.dot(p2, vc_ref[0, :, pv], preferred_element_type=jnp.float32))
        o = o[:, par * LANES:(par + 1) * LANES]
        l = jnp.sum(jnp.where(ones_lane[par], o, 0.0), axis=-1, keepdims=True)
        outs.append(o / l)
    for p in range(N_PAIRS):
        o_ref[0, rows, p * LANES:(p + 1) * LANES] = jnp.where(
            lo, outs[2 * p], outs[2 * p + 1]).astype(jnp.bfloat16)


def _attn_call(q, k_lat, k_ctx, v_lat, v_ctx, tq):
    bsz, _, seq, _ = q.shape
    assert seq % tq == 0 and tq % Q_SUB == 0, (seq, tq)
    n_ctx = k_ctx.shape[2]
    return pl.pallas_call(
        _attn_kernel,
        grid=(bsz, seq // tq),
        in_specs=[pl.BlockSpec((1, MLA_HEADS, tq, HEAD_PAD), lambda b, i: (b, 0, i, 0)),
                  pl.BlockSpec((1, MLA_HEADS, seq, HEAD_PAD), lambda b, i: (b, 0, 0, 0)),
                  pl.BlockSpec((1, MLA_HEADS, n_ctx, HEAD_PAD), lambda b, i: (b, 0, 0, 0)),
                  pl.BlockSpec((1, seq, MLA_HEADS * LANES), lambda b, i: (b, 0, 0)),
                  pl.BlockSpec((1, n_ctx, MLA_HEADS * LANES), lambda b, i: (b, 0, 0))],
        out_specs=pl.BlockSpec((1, tq, MLA_HEADS * V_HEAD), lambda b, i: (b, i, 0)),
        out_shape=jax.ShapeDtypeStruct((bsz, seq, MLA_HEADS * V_HEAD), jnp.bfloat16),
        compiler_params=pltpu.CompilerParams(
            dimension_semantics=("arbitrary", "arbitrary"),
            vmem_limit_bytes=VMEM_LIMIT),
        name="attention",
    )(q, k_lat, k_ctx, v_lat, v_ctx)


def _out_ffn_kernel(x_ref, attn_ref, sg_ref, modm_ref, wout_ref, mod_ref, nw_ref,
                    w1_ref, w3_ref, w2_ref, o_ref, h_ref, acc_ref):
    y = (jnp.dot(attn_ref[0], wout_ref[0], preferred_element_type=jnp.float32)
         + jnp.dot(sg_ref[0], wout_ref[1], preferred_element_type=jnp.float32))
    x = x_ref[0] + modm_ref[0, 2] * y
    o_ref[0] = _ffn_core(x, mod_ref, nw_ref, w1_ref, w3_ref, w2_ref, h_ref, acc_ref)


def _out_ffn_call(x, attn, sg, mod, w_out, norm_w, w1, w3, w2, tm):
    bsz, seq, _ = x.shape
    assert seq % tm == 0, (seq, tm)
    half = MLA_HEADS * V_HEAD
    row = lambda w: pl.BlockSpec((1, tm, w), lambda b, i: (b, i, 0))
    return pl.pallas_call(
        _out_ffn_kernel,
        grid=(bsz, seq // tm),
        in_specs=[row(D_MODEL), row(half), row(GMLP_WIDTH), _mod_spec(1, True),
                  _const_spec((2, half, D_MODEL)), _mod_spec(2, True), _const_spec((1, D_MODEL)),
                  _const_spec((D_MODEL, D_FF)), _const_spec((D_MODEL, D_FF)),
                  _const_spec((N_FF, FF_TILE, D_MODEL))],
        out_specs=row(D_MODEL),
        out_shape=jax.ShapeDtypeStruct(x.shape, jnp.float32),
        scratch_shapes=[pltpu.VMEM((tm, D_MODEL), jnp.bfloat16),
                        pltpu.VMEM((tm, D_MODEL), jnp.float32)],
        compiler_params=pltpu.CompilerParams(dimension_semantics=("arbitrary", "arbitrary"),
                                             vmem_limit_bytes=VMEM_LIMIT),
        name="out_ffn",
    )(x, attn, sg, mod, w_out, mod, norm_w, w1, w3, w2)


def _ffn_weights(w1, w3, w2):
    bf = jnp.bfloat16
    return w1.astype(bf), w3.astype(bf), w2.astype(bf).reshape(N_FF, FF_TILE, D_MODEL)


def _rot_cols(w, start, signed=True):
    half = AXIS_DIM // 2
    parts = []
    for blk in range(QK_ROPE // half):
        src = start + (blk + 1) * half if blk % 2 == 0 else start + (blk - 1) * half
        piece = w[..., src:src + half]
        parts.append(-piece if (signed and blk % 2 == 0) else piece)
    return jnp.concatenate(parts, axis=-1)


def _selectors():
    r = np.arange(2 * LANES)[:, None]
    c = np.arange(2 * LANES)[None, :]
    sel_k = ((r < 64) & (c < LANES)) | ((r >= 64) & (r < LANES) & (c >= LANES)) | (
        (r >= LANES) & (r < LANES + QK_ROPE))
    ro = r - LANES
    sel_q = ((r < QK_HEAD) & (c < LANES)) | (
        (r >= LANES) & ((ro < QK_ROPE) | (ro >= 2 * QK_ROPE)) & (c >= LANES))
    sel_v = (r // GMLP_GROUP_DIM) == (c // GMLP_GROUP_DIM)
    return [jnp.asarray(m, jnp.bfloat16) for m in (sel_k, sel_q, sel_v)]


def _mix_weights(norm2_w, w_in, q_a_norm_w, w_uq, kv_a_norm_w, w_ukv, q_norm_w, k_norm_w,
                 v_norm_w, w_s, b_s):
    bf = jnp.bfloat16
    kpe = w_in[:, KV_LORA:KV_COLS]
    kpe_rot = _rot_cols(w_in, KV_LORA)
    w_in_ext = jnp.concatenate([w_in[:, :KV_LORA], kpe, kpe, kpe_rot, kpe_rot,
                                w_in[:, Q_START:]], axis=1).astype(bf)
    ukv = w_ukv.reshape(KV_LORA, MLA_HEADS, QK_NOPE + V_HEAD)
    w_ukv_p = jnp.concatenate([ukv[:, :, :QK_NOPE].reshape(KV_LORA, -1),
                               ukv[:, :, QK_NOPE:].reshape(KV_LORA, -1)], axis=1).astype(bf)
    cols = []
    for h in range(MLA_HEADS):
        base = h * QK_HEAD
        nope = w_uq[:, base:base + QK_NOPE]
        pes = w_uq[:, base + QK_NOPE:base + QK_HEAD]
        rots = _rot_cols(w_uq, base + QK_NOPE)
        cols += [nope, pes, rots] if h % 2 == 0 else [pes, rots, nope]
    w_uq_ext = jnp.concatenate(cols, axis=1).astype(bf)
    qn, qp, qr = q_norm_w[:QK_NOPE], q_norm_w[QK_NOPE:], _rot_cols(q_norm_w, QK_NOPE, signed=False)
    wq = jnp.concatenate([qn, qp, qr, qp, qr, qn])[None]
    kn, kp, kr = k_norm_w[:QK_NOPE], k_norm_w[QK_NOPE:], _rot_cols(k_norm_w, QK_NOPE, signed=False)
    sel_k, sel_q, sel_v = _selectors()
    return dict(
        sel_k=sel_k, sel_q=sel_q, sel_v=sel_v,
        norm2=norm2_w[None], w_in=w_in_ext, kvn=kv_a_norm_w[None], w_ukv=w_ukv_p,
        wk_nope=jnp.concatenate([kn, kn])[None],
        wk_ab=jnp.concatenate([kp, kp, kr, kr])[None],
        qan=q_a_norm_w[None], w_uq=w_uq_ext, wq=wq,
        vnw=v_norm_w.reshape(1, GMLP_WIDTH),
        ws=w_s.astype(bf).reshape(GMLP_GROUPS // 2, 2, CHUNK, CHUNK).transpose(0, 2, 1, 3
                                   ).reshape(GMLP_GROUPS // 2, CHUNK, 2 * CHUNK),
        bs=jnp.broadcast_to(b_s.T[:, :, None], (CHUNK, GMLP_GROUPS, GMLP_GROUP_DIM)
                            ).reshape(CHUNK, GMLP_WIDTH),
    )


def _rope_tables(seq, n_ctx):
    f32 = jnp.float32
    rows_n = seq // GRID_W
    rows = jnp.repeat(jnp.arange(rows_n, dtype=f32), GRID_W)
    cols = jnp.tile(jnp.arange(GRID_W, dtype=f32), rows_n)
    inv = ROPE_BASE ** (-jnp.arange(0, AXIS_DIM, 2, dtype=f32) / AXIS_DIM)
    ang_r = rows[:, None] * inv
    ang_c = cols[:, None] * inv
    ang = jnp.concatenate([ang_r, ang_r, ang_c, ang_c], axis=-1)
    cos, sin = jnp.cos(ang), jnp.sin(ang)
    one = jnp.ones((seq, 64), f32)
    lat = dict(k=jnp.concatenate([cos, cos, sin, sin], 1),
               q=jnp.concatenate([one, cos, sin, cos, sin, one], 1))
    ctx = dict(k=jnp.concatenate([jnp.ones((n_ctx, 64), f32), jnp.zeros((n_ctx, 64), f32)], 1))
    return lat, ctx


def kernel(x, c, ctx, c_ctx, w_ada, b_ada, norm1_w, ffn1_w1, ffn1_w3, ffn1_w2, norm2_w, w_in,
           q_a_norm_w, w_uq, kv_a_norm_w, w_ukv, q_norm_w, k_norm_w, v_norm_w, w_s, b_s, w_out,
           norm3_w, ffn2_w1, ffn2_w3, ffn2_w2):
    bsz, seq, _ = x.shape
    n_ctx = ctx.shape[1]
    rows = -(-(bsz + 1) // 8) * 8
    cc = jnp.concatenate([c, c_ctx[None], jnp.zeros((rows - bsz - 1, D_MODEL), jnp.float32)], 0)
    mod = _ada_call(cc, w_ada[0], b_ada[0][None]).reshape(rows, N_MOD, 1, D_MODEL)
    mod_ctx = mod[bsz:bsz + 1]

    f1 = _ffn_weights(ffn1_w1[0], ffn1_w3[0], ffn1_w2[0])
    f2 = _ffn_weights(ffn2_w1[0], ffn2_w3[0], ffn2_w2[0])
    wts = _mix_weights(norm2_w[0], w_in[0], q_a_norm_w[0], w_uq[0], kv_a_norm_w[0], w_ukv[0],
                       q_norm_w[0], k_norm_w[0], v_norm_w[0], w_s[0], b_s[0])
    tabs_lat, tabs_ctx = _rope_tables(seq, n_ctx)

    x1 = _ffn_call(x, mod, 0, True, norm1_w, *f1, tm=ROW_TILE)
    ctx1 = _ffn_call(ctx.reshape(1, bsz * n_ctx, D_MODEL), mod_ctx, 0, False, norm1_w, *f1,
                     tm=ROW_TILE).reshape(bsz, n_ctx, D_MODEL)
    k_lat, v_lat, q, sg = _prep_call(x1, mod, True, wts, tabs_lat, tm=PREP_TILE, with_q=True)
    k_ctx, v_ctx = _prep_call(ctx1, mod_ctx, False, wts, tabs_ctx, tm=n_ctx, with_q=False)
    attn = _attn_call(q, k_lat, k_ctx, v_lat, v_ctx, tq=Q_TILE)
    w_out_r = w_out[0].astype(jnp.bfloat16).reshape(2, MLA_HEADS * V_HEAD, D_MODEL)
    return _out_ffn_call(x1, attn, sg, mod, w_out_r, norm3_w, *f2, tm=ROW_TILE)
```

```python
import numpy as np
import jax
import jax.numpy as jnp
from jax import lax
from jax.experimental import pallas as pl
from jax.experimental.pallas import tpu as pltpu

D_MODEL = 1024
GRID_W = 64
MLA_HEADS = 8
QK_NOPE = 64
QK_ROPE = 32
QK_HEAD = QK_NOPE + QK_ROPE
V_HEAD = 64
Q_LORA = 256
KV_LORA = 128
AXIS_DIM = QK_ROPE // 2
ROPE_BASE = 10000.0
GMLP_GROUPS = 8
GMLP_GROUP_DIM = 64
GMLP_WIDTH = GMLP_GROUPS * GMLP_GROUP_DIM
CHUNK = 128
KV_COLS = KV_LORA + QK_ROPE
Q_START = KV_COLS
U_START = KV_COLS + Q_LORA
V_START = U_START + GMLP_WIDTH
IN_COLS = V_START + GMLP_WIDTH
D_FF = 2816
N_MOD = 9
EPS = 1e-6

F8 = jnp.float8_e4m3fn
F8_MAX = 448.0
P_SHIFT = 7.0
F8_LOGIT_BOUND = 16.0
LANES = 128
HEAD_PAD = LANES
N_PAIRS = MLA_HEADS // 2
VMEM_LIMIT = 56 * 1024 * 1024

EXT_KV = 0
EXT_AB = 128
EXT_Q = 256
EXT_U = EXT_Q + Q_LORA
EXT_V = EXT_U + GMLP_WIDTH
EXT_COLS = EXT_V + GMLP_WIDTH
EXT_KV_ONLY = EXT_Q

FF_TILE = 256
N_FF = D_FF // FF_TILE
ROW_TILE = 1024
PREP_TILE = 512
Q_TILE = 1024
Q_SUB = 512


def _rms_scale(x, n):
    return lax.rsqrt(jnp.sum(x * x, axis=-1, keepdims=True) * (1.0 / n) + EPS)


def _silu(a):
    return a / (1.0 + jnp.exp(-a))


def _gelu_tanh(x):
    c = np.float32(np.sqrt(2.0 / np.pi))
    t = jnp.tanh(x * (c + np.float32(c * 0.044715) * (x * x)))
    return x * (0.5 + 0.5 * t)


def _ada_kernel(c_ref, w_ref, b_ref, o_ref):
    s = _silu(c_ref[...]).astype(jnp.bfloat16)
    o_ref[...] = jnp.dot(s, w_ref[...].astype(jnp.bfloat16),
                         preferred_element_type=jnp.float32) + b_ref[...]


def _ada_call(cc, w_ada, b_ada):
    rows = cc.shape[0]
    n = w_ada.shape[1]
    tn = 1024
    return pl.pallas_call(
        _ada_kernel,
        grid=(n // tn,),
        in_specs=[pl.BlockSpec((rows, D_MODEL), lambda j: (0, 0)),
                  pl.BlockSpec((D_MODEL, tn), lambda j: (0, j)),
                  pl.BlockSpec((1, tn), lambda j: (0, j))],
        out_specs=pl.BlockSpec((rows, tn), lambda j: (0, j)),
        out_shape=jax.ShapeDtypeStruct((rows, n), jnp.float32),
        compiler_params=pltpu.CompilerParams(dimension_semantics=("arbitrary",),
                                             vmem_limit_bytes=VMEM_LIMIT),
        name="adaln",
    )(cc, w_ada, b_ada)


def _ffn_core(x, mod_ref, nw_ref, w1_ref, w3_ref, w2_ref, h_ref, acc_ref):
    shift, scale, gate = mod_ref[0, 0], mod_ref[0, 1], mod_ref[0, 2]
    h_ref[...] = (x * _rms_scale(x, D_MODEL) * (nw_ref[...] * (1.0 + scale)) + shift
                  ).astype(jnp.bfloat16)
    acc_ref[...] = jnp.zeros_like(acc_ref)

    for j in range(N_FF):
        hb = h_ref[...]
        cols = slice(j * FF_TILE, (j + 1) * FF_TILE)
        a = jnp.dot(hb, w1_ref[:, cols], preferred_element_type=jnp.float32)
        b = jnp.dot(hb, w3_ref[:, cols], preferred_element_type=jnp.float32)
        g = (_silu(a) * b).astype(jnp.bfloat16)
        acc_ref[...] += jnp.dot(g, w2_ref[j], preferred_element_type=jnp.float32)
    return x + (0.5 * gate) * acc_ref[...]


def _ffn_kernel(x_ref, mod_ref, nw_ref, w1_ref, w3_ref, w2_ref, o_ref, h_ref, acc_ref):
    o_ref[0] = _ffn_core(x_ref[0], mod_ref, nw_ref, w1_ref, w3_ref, w2_ref, h_ref, acc_ref)


def _const_spec(shape):
    nd = len(shape)
    return pl.BlockSpec(shape, lambda *_: (0,) * nd, pipeline_mode=pl.Buffered(1))


def _mod_spec(mod_block, per_batch):
    if per_batch:
        return pl.BlockSpec((1, 3, 1, D_MODEL), lambda b, i: (b, mod_block, 0, 0))
    return pl.BlockSpec((1, 3, 1, D_MODEL), lambda b, i: (0, mod_block, 0, 0))


def _ffn_call(x, mod, mod_block, per_batch, norm_w, w1, w3, w2, tm):
    bsz, seq, _ = x.shape
    assert seq % tm == 0, (seq, tm)
    return pl.pallas_call(
        _ffn_kernel,
        grid=(bsz, seq // tm),
        in_specs=[pl.BlockSpec((1, tm, D_MODEL), lambda b, i: (b, i, 0)),
                  _mod_spec(mod_block, per_batch),
                  _const_spec((1, D_MODEL)),
                  _const_spec((D_MODEL, D_FF)),
                  _const_spec((D_MODEL, D_FF)),
                  _const_spec((N_FF, FF_TILE, D_MODEL))],
        out_specs=pl.BlockSpec((1, tm, D_MODEL), lambda b, i: (b, i, 0)),
        out_shape=jax.ShapeDtypeStruct(x.shape, jnp.float32),
        scratch_shapes=[pltpu.VMEM((tm, D_MODEL), jnp.bfloat16),
                        pltpu.VMEM((tm, D_MODEL), jnp.float32)],
        compiler_params=pltpu.CompilerParams(dimension_semantics=("arbitrary", "arbitrary"),
                                             vmem_limit_bytes=VMEM_LIMIT),
        name="ffn",
    )(x, mod, norm_w, w1, w3, w2)


def _lane_iota(shape):
    return lax.broadcasted_iota(jnp.int32, shape, len(shape) - 1)


def _sq_bf16(x):
    return (x * x).astype(jnp.bfloat16)


def _seg_rms(sq, sel_ref, n):
    ss = jnp.dot(sq, sel_ref[...], preferred_element_type=jnp.float32)
    return lax.rsqrt(ss + n * EPS)


def _clamp(x, bound, enabled):
    return jnp.clip(x, -bound, bound) if enabled else x


def _kv_prep(proj, kvn_ref, wukv_ref, wkn_ref, wkab_ref, tk_ref, selk_ref, k_ref, v_ref):
    tm = proj.shape[0]
    ckv = proj[:, EXT_KV:EXT_KV + KV_LORA]
    ckv = (ckv * _rms_scale(ckv, KV_LORA) * kvn_ref[...]).astype(jnp.bfloat16)
    kv = jnp.dot(ckv, wukv_ref[...], preferred_element_type=jnp.float32)
    lane = _lane_iota((tm, LANES))
    lo = lane < 64
    ab = proj[:, EXT_AB:EXT_AB + LANES]
    root_n = np.float32(np.sqrt(QK_HEAD))
    f8 = k_ref.dtype == F8
    g_rope = _clamp(wkab_ref[...] * root_n, 0.5 * F8_MAX, f8)
    g_nope = _clamp(wkn_ref[...] * root_n, F8_MAX, f8)
    t = ab * (g_rope * tk_ref[...])
    rope = t + pltpu.roll(t, 64, axis=1)
    ab_sq = _sq_bf16(ab)
    one_e = jnp.where(lane == V_HEAD, 1.0, 0.0)
    one_o = jnp.where(lane == 0, 1.0, 0.0)
    for p in range(N_PAIRS):
        vp = kv[:, MLA_HEADS * QK_NOPE + p * LANES:MLA_HEADS * QK_NOPE + (p + 1) * LANES]
        vp = _clamp(vp, F8_MAX, f8)
        v_ref[0, :, (2 * p) * LANES:(2 * p + 1) * LANES] = jnp.where(lo, vp, one_e).astype(v_ref.dtype)
        v_ref[0, :, (2 * p + 1) * LANES:(2 * p + 2) * LANES] = jnp.where(lo, one_o, vp).astype(v_ref.dtype)
        kp = kv[:, p * LANES:(p + 1) * LANES]
        r = _seg_rms(jnp.concatenate([_sq_bf16(kp), ab_sq], axis=1), selk_ref, QK_HEAD)
        kw = kp * g_nope
        k_ref[0, 2 * p] = (jnp.where(lo, kw, rope) * r[:, :LANES]).astype(k_ref.dtype)
        k_ref[0, 2 * p + 1] = (jnp.where(lo, rope, kw) * r[:, LANES:]).astype(k_ref.dtype)


def _modulated_proj(x, mod_ref, nw_ref, win_ref):
    shift, scale = mod_ref[0, 0], mod_ref[0, 1]
    h = (x * _rms_scale(x, D_MODEL) * (nw_ref[...] * (1.0 + scale)) + shift).astype(jnp.bfloat16)
    return jnp.dot(h, win_ref[...], preferred_element_type=jnp.float32)


def _prep_kernel(x_ref, mod_ref, nw_ref, win_ref, kvn_ref, wukv_ref, wkn_ref, wkab_ref, tk_ref,
                 selk_ref, qan_ref, wuq_ref, wqn_ref, tq_ref, selq_ref, vnw_ref, selv_ref,
                 ws_ref, bs_ref, k_ref, v_ref, q_ref, sg_ref):
    tm = x_ref.shape[1]
    proj = _modulated_proj(x_ref[0], mod_ref, nw_ref, win_ref)
    _kv_prep(proj, kvn_ref, wukv_ref, wkn_ref, wkab_ref, tk_ref, selk_ref, k_ref, v_ref)

    cq = proj[:, EXT_Q:EXT_Q + Q_LORA]
    cq = (cq * _rms_scale(cq, Q_LORA) * qan_ref[...]).astype(jnp.bfloat16)
    qall = jnp.dot(cq, wuq_ref[...], preferred_element_type=jnp.float32)
    lane = _lane_iota((tm, LANES))
    gq = _clamp(wqn_ref[...] * np.float32(np.log2(np.e)), F8_MAX, q_ref.dtype == F8)
    tabs = [gq[:, par * LANES:(par + 1) * LANES] * tq_ref[:, par * LANES:(par + 1) * LANES]
            for par in range(2)]
    for p in range(N_PAIRS):
        qp = qall[:, 2 * p * LANES:(2 * p + 2) * LANES]
        r = _seg_rms(_sq_bf16(qp), selq_ref, QK_HEAD)
        for par in range(2):
            hl = slice(par * LANES, (par + 1) * LANES)
            q_ref[0, 2 * p + par] = (qp[:, hl] * r[:, hl] * tabs[par]).astype(q_ref.dtype)

    u = _gelu_tanh(proj[:, EXT_U:EXT_U + GMLP_WIDTH])
    v = _gelu_tanh(proj[:, EXT_V:EXT_V + GMLP_WIDTH])
    lo = lane < 64
    vn_tiles = []
    for p in range(GMLP_GROUPS // 2):
        if p % 2 == 0:
            v2 = v[:, p * LANES:(p + 2) * LANES]
            vn2 = (v2 * _seg_rms(_sq_bf16(v2), selv_ref, GMLP_GROUP_DIM)
                   * (vnw_ref[:, p * LANES:(p + 2) * LANES] * np.float32(np.sqrt(GMLP_GROUP_DIM))))
        vn = vn2[:, (p % 2) * LANES:(p % 2 + 1) * LANES]
        vn_tiles.append((jnp.where(lo, vn, 0.0).astype(jnp.bfloat16),
                         jnp.where(lo, 0.0, vn).astype(jnp.bfloat16)))
    for c in range(tm // CHUNK):
        rows = slice(c * CHUNK, (c + 1) * CHUNK)
        for p in range(GMLP_GROUPS // 2):
            vblk = jnp.concatenate([vn_tiles[p][0][rows], vn_tiles[p][1][rows]], axis=0)
            s = (jnp.dot(ws_ref[p], vblk, preferred_element_type=jnp.float32)
                 + bs_ref[:, p * LANES:(p + 1) * LANES])
            sg_ref[0, rows, p * LANES:(p + 1) * LANES] = (
                u[rows, p * LANES:(p + 1) * LANES] * s).astype(jnp.bfloat16)


def _kvonly_kernel(x_ref, mod_ref, nw_ref, win_ref, kvn_ref, wukv_ref, wkn_ref, wkab_ref, tk_ref,
                   selk_ref, k_ref, v_ref):
    proj = _modulated_proj(x_ref[0], mod_ref, nw_ref, win_ref)
    _kv_prep(proj, kvn_ref, wukv_ref, wkn_ref, wkab_ref, tk_ref, selk_ref, k_ref, v_ref)


def _prep_call(x, mod, per_batch, wts, tabs, tm, with_q, op_dtype):
    bsz, seq, _ = x.shape
    assert seq % tm == 0 and tm % CHUNK == 0, (seq, tm)
    x_spec = pl.BlockSpec((1, tm, D_MODEL), lambda b, i: (b, i, 0))
    tab_spec = lambda w: pl.BlockSpec((tm, w), lambda b, i: (i, 0))
    ncols = EXT_COLS if with_q else EXT_KV_ONLY
    kv_specs = [x_spec, _mod_spec(1, per_batch), _const_spec((1, D_MODEL)),
                _const_spec((D_MODEL, ncols)), _const_spec((1, KV_LORA)),
                _const_spec((KV_LORA, MLA_HEADS * LANES)), _const_spec((1, LANES)),
                _const_spec((1, LANES)), tab_spec(LANES), _const_spec((2 * LANES, 2 * LANES))]
    kv_args = [x, mod, wts["norm2"], wts["w_in"] if with_q else wts["w_in"][:, :EXT_KV_ONLY],
               wts["kvn"], wts["w_ukv"], wts["wk_nope"], wts["wk_ab"], tabs["k"], wts["sel_k"]]
    k_shape = jax.ShapeDtypeStruct((bsz, MLA_HEADS, seq, HEAD_PAD), op_dtype)
    v_shape = jax.ShapeDtypeStruct((bsz, seq, MLA_HEADS * LANES), op_dtype)
    sg_shape = jax.ShapeDtypeStruct((bsz, seq, GMLP_WIDTH), jnp.bfloat16)
    k_spec = pl.BlockSpec((1, MLA_HEADS, tm, HEAD_PAD), lambda b, i: (b, 0, i, 0))
    v_spec = pl.BlockSpec((1, tm, MLA_HEADS * LANES), lambda b, i: (b, i, 0))
    sg_spec = pl.BlockSpec((1, tm, GMLP_WIDTH), lambda b, i: (b, i, 0))
    params = pltpu.CompilerParams(dimension_semantics=("arbitrary", "arbitrary"),
                                  vmem_limit_bytes=VMEM_LIMIT)
    if not with_q:
        return pl.pallas_call(
            _kvonly_kernel, grid=(bsz, seq // tm), in_specs=kv_specs,
            out_specs=[k_spec, v_spec], out_shape=[k_shape, v_shape],
            compiler_params=params, name="kv_prep")(*kv_args)
    q_specs = [_const_spec((1, Q_LORA)), _const_spec((Q_LORA, MLA_HEADS * LANES)),
               _const_spec((1, 2 * LANES)), tab_spec(2 * LANES), _const_spec((2 * LANES, 2 * LANES)),
               _const_spec((1, GMLP_WIDTH)), _const_spec((2 * LANES, 2 * LANES)),
               _const_spec((GMLP_GROUPS // 2, CHUNK, 2 * CHUNK)), _const_spec((CHUNK, GMLP_WIDTH))]
    q_args = [wts["qan"], wts["w_uq"], wts["wq"], tabs["q"], wts["sel_q"], wts["vnw"], wts["sel_v"],
              wts["ws"], wts["bs"]]
    return pl.pallas_call(
        _prep_kernel, grid=(bsz, seq // tm), in_specs=kv_specs + q_specs,
        out_specs=[k_spec, v_spec, k_spec, sg_spec],
        out_shape=[k_shape, v_shape, k_shape, sg_shape],
        compiler_params=params, name="mix_prep")(*kv_args, *q_args)


def _attn_kernel(q_ref, kl_ref, kc_ref, vl_ref, vc_ref, o_ref):
    for r0 in range(0, q_ref.shape[2], Q_SUB):
        _attn_rows(slice(r0, r0 + Q_SUB), q_ref, kl_ref, kc_ref, vl_ref, vc_ref, o_ref)


def _attn_rows(rows, q_ref, kl_ref, kc_ref, vl_ref, vc_ref, o_ref):
    nt = (((1,), (1,)), ((), ()))
    lane = _lane_iota((Q_SUB, LANES))
    lo = lane < 64
    ones_lane = (lane == V_HEAD, lane == 0)
    op_dtype = vl_ref.dtype
    p_shift = P_SHIFT if op_dtype == F8 else 0.0
    outs = []
    for h in range(MLA_HEADS):
        q = q_ref[0, h, rows]
        par = h % 2
        pv = slice((h - par) * LANES, (h - par + 2) * LANES)
        s1 = lax.dot_general(q, kl_ref[0, h], nt, preferred_element_type=jnp.float32)
        s2 = lax.dot_general(q, kc_ref[0, h], nt, preferred_element_type=jnp.float32)
        m = jnp.maximum(jnp.max(s1, axis=-1, keepdims=True),
                        jnp.max(s2, axis=-1, keepdims=True)) - p_shift
        p1 = jnp.exp2(s1 - m).astype(op_dtype)
        p2 = jnp.exp2(s2 - m).astype(op_dtype)
        o = (jnp.dot(p1, vl_ref[0, :, pv], preferred_element_type=jnp.float32)
             + jnp.dot(p2, vc_ref[0, :, pv], preferred_element_type=jnp.float32))
        o = o[:, par * LANES:(par + 1) * LANES]
        l = jnp.sum(jnp.where(ones_lane[par], o, 0.0), axis=-1, keepdims=True)
        outs.append(o / l)
    for p in range(N_PAIRS):
        o_ref[0, rows, p * LANES:(p + 1) * LANES] = jnp.where(
            lo, outs[2 * p], outs[2 * p + 1]).astype(jnp.bfloat16)


def _attn_call(q, k_lat, k_ctx, v_lat, v_ctx, tq):
    bsz, _, seq, _ = q.shape
    assert seq % tq == 0 and tq % Q_SUB == 0, (seq, tq)
    n_ctx = k_ctx.shape[2]
    return pl.pallas_call(
        _attn_kernel,
        grid=(bsz, seq // tq),
        in_specs=[pl.BlockSpec((1, MLA_HEADS, tq, HEAD_PAD), lambda b, i: (b, 0, i, 0)),
                  pl.BlockSpec((1, MLA_HEADS, seq, HEAD_PAD), lambda b, i: (b, 0, 0, 0)),
                  pl.BlockSpec((1, MLA_HEADS, n_ctx, HEAD_PAD), lambda b, i: (b, 0, 0, 0)),
                  pl.BlockSpec((1, seq, MLA_HEADS * LANES), lambda b, i: (b, 0, 0)),
                  pl.BlockSpec((1, n_ctx, MLA_HEADS * LANES), lambda b, i: (b, 0, 0))],
        out_specs=pl.BlockSpec((1, tq, MLA_HEADS * V_HEAD), lambda b, i: (b, i, 0)),
        out_shape=jax.ShapeDtypeStruct((bsz, seq, MLA_HEADS * V_HEAD), jnp.bfloat16),
        compiler_params=pltpu.CompilerParams(
            dimension_semantics=("arbitrary", "arbitrary"),
            vmem_limit_bytes=VMEM_LIMIT),
        name="attention",
    )(q, k_lat, k_ctx, v_lat, v_ctx)


def _out_ffn_kernel(x_ref, attn_ref, sg_ref, modm_ref, wout_ref, mod_ref, nw_ref,
                    w1_ref, w3_ref, w2_ref, o_ref, h_ref, acc_ref):
    y = (jnp.dot(attn_ref[0], wout_ref[0], preferred_element_type=jnp.float32)
         + jnp.dot(sg_ref[0], wout_ref[1], preferred_element_type=jnp.float32))
    x = x_ref[0] + modm_ref[0, 2] * y
    o_ref[0] = _ffn_core(x, mod_ref, nw_ref, w1_ref, w3_ref, w2_ref, h_ref, acc_ref)


def _out_ffn_call(x, attn, sg, mod, w_out, norm_w, w1, w3, w2, tm):
    bsz, seq, _ = x.shape
    assert seq % tm == 0, (seq, tm)
    half = MLA_HEADS * V_HEAD
    row = lambda w: pl.BlockSpec((1, tm, w), lambda b, i: (b, i, 0))
    return pl.pallas_call(
        _out_ffn_kernel,
        grid=(bsz, seq // tm),
        in_specs=[row(D_MODEL), row(half), row(GMLP_WIDTH), _mod_spec(1, True),
                  _const_spec((2, half, D_MODEL)), _mod_spec(2, True), _const_spec((1, D_MODEL)),
                  _const_spec((D_MODEL, D_FF)), _const_spec((D_MODEL, D_FF)),
                  _const_spec((N_FF, FF_TILE, D_MODEL))],
        out_specs=row(D_MODEL),
        out_shape=jax.ShapeDtypeStruct(x.shape, jnp.float32),
        scratch_shapes=[pltpu.VMEM((tm, D_MODEL), jnp.bfloat16),
                        pltpu.VMEM((tm, D_MODEL), jnp.float32)],
        compiler_params=pltpu.CompilerParams(dimension_semantics=("arbitrary", "arbitrary"),
                                             vmem_limit_bytes=VMEM_LIMIT),
        name="out_ffn",
    )(x, attn, sg, mod, w_out, mod, norm_w, w1, w3, w2)


def _ffn_weights(w1, w3, w2):
    bf = jnp.bfloat16
    return w1.astype(bf), w3.astype(bf), w2.astype(bf).reshape(N_FF, FF_TILE, D_MODEL)


def _rot_cols(w, start, signed=True):
    half = AXIS_DIM // 2
    parts = []
    for blk in range(QK_ROPE // half):
        src = start + (blk + 1) * half if blk % 2 == 0 else start + (blk - 1) * half
        piece = w[..., src:src + half]
        parts.append(-piece if (signed and blk % 2 == 0) else piece)
    return jnp.concatenate(parts, axis=-1)


def _selectors():
    r = np.arange(2 * LANES)[:, None]
    c = np.arange(2 * LANES)[None, :]
    sel_k = ((r < 64) & (c < LANES)) | ((r >= 64) & (r < LANES) & (c >= LANES)) | (
        (r >= LANES) & (r < LANES + QK_ROPE))
    ro = r - LANES
    sel_q = ((r < QK_HEAD) & (c < LANES)) | (
        (r >= LANES) & ((ro < QK_ROPE) | (ro >= 2 * QK_ROPE)) & (c >= LANES))
    sel_v = (r // GMLP_GROUP_DIM) == (c // GMLP_GROUP_DIM)
    return [jnp.asarray(m, jnp.bfloat16) for m in (sel_k, sel_q, sel_v)]


def _mix_weights(norm2_w, w_in, q_a_norm_w, w_uq, kv_a_norm_w, w_ukv, q_norm_w, k_norm_w,
                 v_norm_w, w_s, b_s):
    bf = jnp.bfloat16
    kpe = w_in[:, KV_LORA:KV_COLS]
    kpe_rot = _rot_cols(w_in, KV_LORA)
    w_in_ext = jnp.concatenate([w_in[:, :KV_LORA], kpe, kpe, kpe_rot, kpe_rot,
                                w_in[:, Q_START:]], axis=1).astype(bf)
    ukv = w_ukv.reshape(KV_LORA, MLA_HEADS, QK_NOPE + V_HEAD)
    w_ukv_p = jnp.concatenate([ukv[:, :, :QK_NOPE].reshape(KV_LORA, -1),
                               ukv[:, :, QK_NOPE:].reshape(KV_LORA, -1)], axis=1).astype(bf)
    cols = []
    for h in range(MLA_HEADS):
        base = h * QK_HEAD
        nope = w_uq[:, base:base + QK_NOPE]
        pes = w_uq[:, base + QK_NOPE:base + QK_HEAD]
        rots = _rot_cols(w_uq, base + QK_NOPE)
        cols += [nope, pes, rots] if h % 2 == 0 else [pes, rots, nope]
    w_uq_ext = jnp.concatenate(cols, axis=1).astype(bf)
    qn, qp, qr = q_norm_w[:QK_NOPE], q_norm_w[QK_NOPE:], _rot_cols(q_norm_w, QK_NOPE, signed=False)
    wq = jnp.concatenate([qn, qp, qr, qp, qr, qn])[None]
    kn, kp, kr = k_norm_w[:QK_NOPE], k_norm_w[QK_NOPE:], _rot_cols(k_norm_w, QK_NOPE, signed=False)
    sel_k, sel_q, sel_v = _selectors()
    return dict(
        sel_k=sel_k, sel_q=sel_q, sel_v=sel_v,
        norm2=norm2_w[None], w_in=w_in_ext, kvn=kv_a_norm_w[None], w_ukv=w_ukv_p,
        wk_nope=jnp.concatenate([kn, kn])[None],
        wk_ab=jnp.concatenate([kp, kp, kr, kr])[None],
        qan=q_a_norm_w[None], w_uq=w_uq_ext, wq=wq,
        vnw=v_norm_w.reshape(1, GMLP_WIDTH),
        ws=w_s.astype(bf).reshape(GMLP_GROUPS // 2, 2, CHUNK, CHUNK).transpose(0, 2, 1, 3
                                   ).reshape(GMLP_GROUPS // 2, CHUNK, 2 * CHUNK),
        bs=jnp.broadcast_to(b_s.T[:, :, None], (CHUNK, GMLP_GROUPS, GMLP_GROUP_DIM)
                            ).reshape(CHUNK, GMLP_WIDTH),
    )


def _rope_tables(seq, n_ctx):
    f32 = jnp.float32
    rows_n = seq // GRID_W
    rows = jnp.repeat(jnp.arange(rows_n, dtype=f32), GRID_W)
    cols = jnp.tile(jnp.arange(GRID_W, dtype=f32), rows_n)
    inv = ROPE_BASE ** (-jnp.arange(0, AXIS_DIM, 2, dtype=f32) / AXIS_DIM)
    ang_r = rows[:, None] * inv
    ang_c = cols[:, None] * inv
    ang = jnp.concatenate([ang_r, ang_r, ang_c, ang_c], axis=-1)
    cos, sin = jnp.cos(ang), jnp.sin(ang)
    one = jnp.ones((seq, 64), f32)
    lat = dict(k=jnp.concatenate([cos, cos, sin, sin], 1),
               q=jnp.concatenate([one, cos, sin, cos, sin, one], 1))
    ctx = dict(k=jnp.concatenate([jnp.ones((n_ctx, 64), f32), jnp.zeros((n_ctx, 64), f32)], 1))
    return lat, ctx


def kernel(x, c, ctx, c_ctx, w_ada, b_ada, norm1_w, ffn1_w1, ffn1_w3, ffn1_w2, norm2_w, w_in,
           q_a_norm_w, w_uq, kv_a_norm_w, w_ukv, q_norm_w, k_norm_w, v_norm_w, w_s, b_s, w_out,
           norm3_w, ffn2_w1, ffn2_w3, ffn2_w2):
    bsz, seq, _ = x.shape
    n_ctx = ctx.shape[1]
    rows = -(-(bsz + 1) // 8) * 8
    cc = jnp.concatenate([c, c_ctx[None], jnp.zeros((rows - bsz - 1, D_MODEL), jnp.float32)], 0)
    mod = _ada_call(cc, w_ada[0], b_ada[0][None]).reshape(rows, N_MOD, 1, D_MODEL)
    mod_ctx = mod[bsz:bsz + 1]

    f1 = _ffn_weights(ffn1_w1[0], ffn1_w3[0], ffn1_w2[0])
    f2 = _ffn_weights(ffn2_w1[0], ffn2_w3[0], ffn2_w2[0])
    wts = _mix_weights(norm2_w[0], w_in[0], q_a_norm_w[0], w_uq[0], kv_a_norm_w[0], w_ukv[0],
                       q_norm_w[0], k_norm_w[0], v_norm_w[0], w_s[0], b_s[0])
    tabs_lat, tabs_ctx = _rope_tables(seq, n_ctx)

    x1 = _ffn_call(x, mod, 0, True, norm1_w, *f1, tm=ROW_TILE)
    ctx1 = _ffn_call(ctx.reshape(1, bsz * n_ctx, D_MODEL), mod_ctx, 0, False, norm1_w, *f1,
                     tm=ROW_TILE).reshape(bsz, n_ctx, D_MODEL)

    def mix(op_dtype):
        k_lat, v_lat, q, sg = _prep_call(x1, mod, True, wts, tabs_lat, PREP_TILE, True, op_dtype)
        k_ctx, v_ctx = _prep_call(ctx1, mod_ctx, False, wts, tabs_ctx, n_ctx, False, op_dtype)
        return _attn_call(q, k_lat, k_ctx, v_lat, v_ctx, tq=Q_TILE), sg

    logit_bound = (QK_HEAD ** 0.5) * jnp.max(jnp.abs(q_norm_w[0])) * jnp.max(jnp.abs(k_norm_w[0]))
    attn, sg = lax.cond(logit_bound <= F8_LOGIT_BOUND, lambda: mix(F8), lambda: mix(jnp.bfloat16))
    w_out_r = w_out[0].astype(jnp.bfloat16).reshape(2, MLA_HEADS * V_HEAD, D_MODEL)
    return _out_ffn_call(x1, attn, sg, mod, w_out_r, norm3_w, *f2, tm=ROW_TILE)
```

```python
import numpy as np
import jax
import jax.numpy as jnp
from jax import lax
from jax.experimental import pallas as pl
from jax.experimental.pallas import tpu as pltpu

D_MODEL = 1024
GRID_W = 64
MLA_HEADS = 8
QK_NOPE = 64
QK_ROPE = 32
QK_HEAD = QK_NOPE + QK_ROPE
V_HEAD = 64
Q_LORA = 256
KV_LORA = 128
AXIS_DIM = QK_ROPE // 2
ROPE_BASE = 10000.0
GMLP_GROUPS = 8
GMLP_GROUP_DIM = 64
GMLP_WIDTH = GMLP_GROUPS * GMLP_GROUP_DIM
CHUNK = 128
KV_COLS = KV_LORA + QK_ROPE
Q_START = KV_COLS
U_START = KV_COLS + Q_LORA
V_START = U_START + GMLP_WIDTH
IN_COLS = V_START + GMLP_WIDTH
D_FF = 2816
N_MOD = 9
EPS = 1e-6

F8 = jnp.float8_e4m3fn
F8_MAX = 448.0
P_SHIFT = 7.0
F8_LOGIT_BOUND = 16.0
LANES = 128
HEAD_PAD = LANES
N_PAIRS = MLA_HEADS // 2
VMEM_LIMIT = 56 * 1024 * 1024

EXT_KV = 0
EXT_AB = 128
EXT_Q = 256
EXT_U = EXT_Q + Q_LORA
EXT_V = EXT_U + GMLP_WIDTH
EXT_COLS = EXT_V + GMLP_WIDTH
EXT_KV_ONLY = EXT_Q

FF_TILE = 256
N_FF = D_FF // FF_TILE
ROW_TILE = 1024
PREP_TILE = 512
Q_TILE = 1024
Q_SUB = 512


def _rms_scale(x, n):
    return lax.rsqrt(jnp.sum(x * x, axis=-1, keepdims=True) * (1.0 / n) + EPS)


def _silu(a):
    return a / (1.0 + jnp.exp(-a))


def _gelu_tanh(x):
    c = np.float32(np.sqrt(2.0 / np.pi))
    t = jnp.tanh(x * (c + np.float32(c * 0.044715) * (x * x)))
    return x * (0.5 + 0.5 * t)


def _ada_kernel(c_ref, w_ref, b_ref, o_ref):
    s = _silu(c_ref[...]).astype(jnp.bfloat16)
    o_ref[...] = jnp.dot(s, w_ref[...].astype(jnp.bfloat16),
                         preferred_element_type=jnp.float32) + b_ref[...]


def _ada_call(cc, w_ada, b_ada):
    rows = cc.shape[0]
    n = w_ada.shape[1]
    tn = 1024
    return pl.pallas_call(
        _ada_kernel,
        grid=(n // tn,),
        in_specs=[pl.BlockSpec((rows, D_MODEL), lambda j: (0, 0)),
                  pl.BlockSpec((D_MODEL, tn), lambda j: (0, j)),
                  pl.BlockSpec((1, tn), lambda j: (0, j))],
        out_specs=pl.BlockSpec((rows, tn), lambda j: (0, j)),
        out_shape=jax.ShapeDtypeStruct((rows, n), jnp.float32),
        compiler_params=pltpu.CompilerParams(dimension_semantics=("arbitrary",),
                                             vmem_limit_bytes=VMEM_LIMIT),
        name="adaln",
    )(cc, w_ada, b_ada)


def _ffn_core(x, mod_ref, nw_ref, w1_ref, w3_ref, w2_ref, h_ref, acc_ref):
    shift, scale, gate = mod_ref[0, 0], mod_ref[0, 1], mod_ref[0, 2]
    h_ref[...] = (x * _rms_scale(x, D_MODEL) * (nw_ref[...] * (1.0 + scale)) + shift
                  ).astype(jnp.bfloat16)
    acc_ref[...] = jnp.zeros_like(acc_ref)

    for j in range(N_FF):
        hb = h_ref[...]
        cols = slice(j * FF_TILE, (j + 1) * FF_TILE)
        a = jnp.dot(hb, w1_ref[:, cols], preferred_element_type=jnp.float32)
        b = jnp.dot(hb, w3_ref[:, cols], preferred_element_type=jnp.float32)
        g = (_silu(a) * b).astype(jnp.bfloat16)
        acc_ref[...] += jnp.dot(g, w2_ref[j], preferred_element_type=jnp.float32)
    return x + (0.5 * gate) * acc_ref[...]


def _ffn_kernel(x_ref, mod_ref, nw_ref, w1_ref, w3_ref, w2_ref, o_ref, h_ref, acc_ref):
    o_ref[0] = _ffn_core(x_ref[0], mod_ref, nw_ref, w1_ref, w3_ref, w2_ref, h_ref, acc_ref)


def _const_spec(shape):
    nd = len(shape)
    return pl.BlockSpec(shape, lambda *_: (0,) * nd, pipeline_mode=pl.Buffered(1))


def _mod_spec(mod_block, per_batch):
    if per_batch:
        return pl.BlockSpec((1, 3, 1, D_MODEL), lambda b, i: (b, mod_block, 0, 0))
    return pl.BlockSpec((1, 3, 1, D_MODEL), lambda b, i: (0, mod_block, 0, 0))


def _ffn_call(x, mod, mod_block, per_batch, norm_w, w1, w3, w2, tm):
    bsz, seq, _ = x.shape
    assert seq % tm == 0, (seq, tm)
    return pl.pallas_call(
        _ffn_kernel,
        grid=(bsz, seq // tm),
        in_specs=[pl.BlockSpec((1, tm, D_MODEL), lambda b, i: (b, i, 0)),
                  _mod_spec(mod_block, per_batch),
                  _const_spec((1, D_MODEL)),
                  _const_spec((D_MODEL, D_FF)),
                  _const_spec((D_MODEL, D_FF)),
                  _const_spec((N_FF, FF_TILE, D_MODEL))],
        out_specs=pl.BlockSpec((1, tm, D_MODEL), lambda b, i: (b, i, 0)),
        out_shape=jax.ShapeDtypeStruct(x.shape, jnp.float32),
        scratch_shapes=[pltpu.VMEM((tm, D_MODEL), jnp.bfloat16),
                        pltpu.VMEM((tm, D_MODEL), jnp.float32)],
        compiler_params=pltpu.CompilerParams(dimension_semantics=("arbitrary", "arbitrary"),
                                             vmem_limit_bytes=VMEM_LIMIT),
        name="ffn",
    )(x, mod, norm_w, w1, w3, w2)


def _lane_iota(shape):
    return lax.broadcasted_iota(jnp.int32, shape, len(shape) - 1)


def _sq_bf16(x):
    return (x * x).astype(jnp.bfloat16)


def _seg_rms(sq, sel_ref, n):
    ss = jnp.dot(sq, sel_ref[...], preferred_element_type=jnp.float32)
    return lax.rsqrt(ss + n * EPS)


def _clamp(x, bound, enabled):
    return jnp.clip(x, -bound, bound) if enabled else x


def _kv_prep(proj, kvn_ref, wukv_ref, wkn_ref, wkab_ref, tk_ref, selk_ref, k_ref, v_ref):
    tm = proj.shape[0]
    ckv = proj[:, EXT_KV:EXT_KV + KV_LORA]
    ckv = (ckv * _rms_scale(ckv, KV_LORA) * kvn_ref[...]).astype(jnp.bfloat16)
    kv = jnp.dot(ckv, wukv_ref[...], preferred_element_type=jnp.float32)
    lane = _lane_iota((tm, LANES))
    lo = lane < 64
    ab = proj[:, EXT_AB:EXT_AB + LANES]
    root_n = np.float32(np.sqrt(QK_HEAD))
    f8 = k_ref.dtype == F8
    g_rope = _clamp(wkab_ref[...] * root_n, 0.5 * F8_MAX, f8)
    g_nope = _clamp(wkn_ref[...] * root_n, F8_MAX, f8)
    t = ab * (g_rope * tk_ref[...])
    rope = t + pltpu.roll(t, 64, axis=1)
    ab_sq = _sq_bf16(ab)
    one_e = jnp.where(lane == V_HEAD, 1.0, 0.0)
    one_o = jnp.where(lane == 0, 1.0, 0.0)
    for p in range(N_PAIRS):
        vp = kv[:, MLA_HEADS * QK_NOPE + p * LANES:MLA_HEADS * QK_NOPE + (p + 1) * LANES]
        vp = _clamp(vp, F8_MAX, f8)
        v_ref[0, :, (2 * p) * LANES:(2 * p + 1) * LANES] = jnp.where(lo, vp, one_e).astype(v_ref.dtype)
        v_ref[0, :, (2 * p + 1) * LANES:(2 * p + 2) * LANES] = jnp.where(lo, one_o, vp).astype(v_ref.dtype)
        kp = kv[:, p * LANES:(p + 1) * LANES]
        r = _seg_rms(jnp.concatenate([_sq_bf16(kp), ab_sq], axis=1), selk_ref, QK_HEAD)
        kw = kp * g_nope
        k_ref[0, 2 * p] = (jnp.where(lo, kw, rope) * r[:, :LANES]).astype(k_ref.dtype)
        k_ref[0, 2 * p + 1] = (jnp.where(lo, rope, kw) * r[:, LANES:]).astype(k_ref.dtype)


def _modulated_proj(x, mod_ref, nw_ref, win_ref):
    shift, scale = mod_ref[0, 0], mod_ref[0, 1]
    h = (x * _rms_scale(x, D_MODEL) * (nw_ref[...] * (1.0 + scale)) + shift).astype(jnp.bfloat16)
    return jnp.dot(h, win_ref[...], preferred_element_type=jnp.float32)


def _prep_kernel(x_ref, mod_ref, nw_ref, win_ref, kvn_ref, wukv_ref, wkn_ref, wkab_ref, tk_ref,
                 selk_ref, qan_ref, wuq_ref, wqn_ref, tq_ref, selq_ref, vnw_ref, selv_ref,
                 ws_ref, bs_ref, k_ref, v_ref, q_ref, sg_ref):
    tm = x_ref.shape[1]
    proj = _modulated_proj(x_ref[0], mod_ref, nw_ref, win_ref)
    _kv_prep(proj, kvn_ref, wukv_ref, wkn_ref, wkab_ref, tk_ref, selk_ref, k_ref, v_ref)

    cq = proj[:, EXT_Q:EXT_Q + Q_LORA]
    cq = (cq * _rms_scale(cq, Q_LORA) * qan_ref[...]).astype(jnp.bfloat16)
    qall = jnp.dot(cq, wuq_ref[...], preferred_element_type=jnp.float32)
    lane = _lane_iota((tm, LANES))
    gq = _clamp(wqn_ref[...] * np.float32(np.log2(np.e)), F8_MAX, q_ref.dtype == F8)
    tabs = [gq[:, par * LANES:(par + 1) * LANES] * tq_ref[:, par * LANES:(par + 1) * LANES]
            for par in range(2)]
    for p in range(N_PAIRS):
        qp = qall[:, 2 * p * LANES:(2 * p + 2) * LANES]
        r = _seg_rms(_sq_bf16(qp), selq_ref, QK_HEAD)
        for par in range(2):
            hl = slice(par * LANES, (par + 1) * LANES)
            q_ref[0, 2 * p + par] = (qp[:, hl] * r[:, hl] * tabs[par]).astype(q_ref.dtype)

    u = _gelu_tanh(proj[:, EXT_U:EXT_U + GMLP_WIDTH])
    v = _gelu_tanh(proj[:, EXT_V:EXT_V + GMLP_WIDTH])
    lo = lane < 64
    vn_tiles = []
    for p in range(GMLP_GROUPS // 2):
        if p % 2 == 0:
            v2 = v[:, p * LANES:(p + 2) * LANES]
            vn2 = (v2 * _seg_rms(_sq_bf16(v2), selv_ref, GMLP_GROUP_DIM)
                   * (vnw_ref[:, p * LANES:(p + 2) * LANES] * np.float32(np.sqrt(GMLP_GROUP_DIM))))
        vn = vn2[:, (p % 2) * LANES:(p % 2 + 1) * LANES]
        vn_tiles.append((jnp.where(lo, vn, 0.0).astype(jnp.bfloat16),
                         jnp.where(lo, 0.0, vn).astype(jnp.bfloat16)))
    for c in range(tm // CHUNK):
        rows = slice(c * CHUNK, (c + 1) * CHUNK)
        for p in range(GMLP_GROUPS // 2):
            vblk = jnp.concatenate([vn_tiles[p][0][rows], vn_tiles[p][1][rows]], axis=0)
            s = (jnp.dot(ws_ref[p], vblk, preferred_element_type=jnp.float32)
                 + bs_ref[:, p * LANES:(p + 1) * LANES])
            sg_ref[0, rows, p * LANES:(p + 1) * LANES] = (
                u[rows, p * LANES:(p + 1) * LANES] * s).astype(jnp.bfloat16)


def _kvonly_kernel(x_ref, mod_ref, nw_ref, win_ref, kvn_ref, wukv_ref, wkn_ref, wkab_ref, tk_ref,
                   selk_ref, k_ref, v_ref):
    proj = _modulated_proj(x_ref[0], mod_ref, nw_ref, win_ref)
    _kv_prep(proj, kvn_ref, wukv_ref, wkn_ref, wkab_ref, tk_ref, selk_ref, k_ref, v_ref)


def _prep_call(x, mod, per_batch, wts, tabs, tm, with_q, op_dtype):
    bsz, seq, _ = x.shape
    assert seq % tm == 0 and tm % CHUNK == 0, (seq, tm)
    x_spec = pl.BlockSpec((1, tm, D_MODEL), lambda b, i: (b, i, 0))
    tab_spec = lambda w: pl.BlockSpec((tm, w), lambda b, i: (i, 0))
    ncols = EXT_COLS if with_q else EXT_KV_ONLY
    kv_specs = [x_spec, _mod_spec(1, per_batch), _const_spec((1, D_MODEL)),
                _const_spec((D_MODEL, ncols)), _const_spec((1, KV_LORA)),
                _const_spec((KV_LORA, MLA_HEADS * LANES)), _const_spec((1, LANES)),
                _const_spec((1, LANES)), tab_spec(LANES), _const_spec((2 * LANES, 2 * LANES))]
    kv_args = [x, mod, wts["norm2"], wts["w_in"] if with_q else wts["w_in"][:, :EXT_KV_ONLY],
               wts["kvn"], wts["w_ukv"], wts["wk_nope"], wts["wk_ab"], tabs["k"], wts["sel_k"]]
    k_shape = jax.ShapeDtypeStruct((bsz, MLA_HEADS, seq, HEAD_PAD), op_dtype)
    v_shape = jax.ShapeDtypeStruct((bsz, seq, MLA_HEADS * LANES), op_dtype)
    sg_shape = jax.ShapeDtypeStruct((bsz, seq, GMLP_WIDTH), jnp.bfloat16)
    k_spec = pl.BlockSpec((1, MLA_HEADS, tm, HEAD_PAD), lambda b, i: (b, 0, i, 0))
    v_spec = pl.BlockSpec((1, tm, MLA_HEADS * LANES), lambda b, i: (b, i, 0))
    sg_spec = pl.BlockSpec((1, tm, GMLP_WIDTH), lambda b, i: (b, i, 0))
    params = pltpu.CompilerParams(dimension_semantics=("arbitrary", "arbitrary"),
                                  vmem_limit_bytes=VMEM_LIMIT)
    if not with_q:
        return pl.pallas_call(
            _kvonly_kernel, grid=(bsz, seq // tm), in_specs=kv_specs,
            out_specs=[k_spec, v_spec], out_shape=[k_shape, v_shape],
            compiler_params=params, name="kv_prep")(*kv_args)
    q_specs = [_const_spec((1, Q_LORA)), _const_spec((Q_LORA, MLA_HEADS * LANES)),
               _const_spec((1, 2 * LANES)), tab_spec(2 * LANES), _const_spec((2 * LANES, 2 * LANES)),
               _const_spec((1, GMLP_WIDTH)), _const_spec((2 * LANES, 2 * LANES)),
               _const_spec((GMLP_GROUPS // 2, CHUNK, 2 * CHUNK)), _const_spec((CHUNK, GMLP_WIDTH))]
    q_args = [wts["qan"], wts["w_uq"], wts["wq"], tabs["q"], wts["sel_q"], wts["vnw"], wts["sel_v"],
              wts["ws"], wts["bs"]]
    return pl.pallas_call(
        _prep_kernel, grid=(bsz, seq // tm), in_specs=kv_specs + q_specs,
        out_specs=[k_spec, v_spec, k_spec, sg_spec],
        out_shape=[k_shape, v_shape, k_shape, sg_shape],
        compiler_params=params, name="mix_prep")(*kv_args, *q_args)


def _attn_kernel(q_ref, kl_ref, kc_ref, vl_ref, vc_ref, *rest):
    n_side = (len(rest) - 1) // 2
    o_ref = rest[n_side]
    for src, dst in zip(rest[:n_side], rest[n_side + 1:]):
        dst[...] = src[...].astype(dst.dtype)
    for r0 in range(0, q_ref.shape[2], Q_SUB):
        _attn_rows(slice(r0, r0 + Q_SUB), q_ref, kl_ref, kc_ref, vl_ref, vc_ref, o_ref)


def _attn_rows(rows, q_ref, kl_ref, kc_ref, vl_ref, vc_ref, o_ref):
    nt = (((1,), (1,)), ((), ()))
    lane = _lane_iota((Q_SUB, LANES))
    lo = lane < 64
    ones_lane = (lane == V_HEAD, lane == 0)
    op_dtype = vl_ref.dtype
    p_shift = P_SHIFT if op_dtype == F8 else 0.0
    outs = []
    for h in range(MLA_HEADS):
        q = q_ref[0, h, rows]
        par = h % 2
        pv = slice((h - par) * LANES, (h - par + 2) * LANES)
        s1 = lax.dot_general(q, kl_ref[0, h], nt, preferred_element_type=jnp.float32)
        s2 = lax.dot_general(q, kc_ref[0, h], nt, preferred_element_type=jnp.float32)
        m = jnp.maximum(jnp.max(s1, axis=-1, keepdims=True),
                        jnp.max(s2, axis=-1, keepdims=True)) - p_shift
        p1 = jnp.exp2(s1 - m).astype(op_dtype)
        p2 = jnp.exp2(s2 - m).astype(op_dtype)
        o = (jnp.dot(p1, vl_ref[0, :, pv], preferred_element_type=jnp.float32)
             + jnp.dot(p2, vc_ref[0, :, pv], preferred_element_type=jnp.float32))
        o = o[:, par * LANES:(par + 1) * LANES]
        l = jnp.sum(jnp.where(ones_lane[par], o, 0.0), axis=-1, keepdims=True)
        outs.append(o / l)
    for p in range(N_PAIRS):
        o_ref[0, rows, p * LANES:(p + 1) * LANES] = jnp.where(
            lo, outs[2 * p], outs[2 * p + 1]).astype(jnp.bfloat16)


def _side_cast_specs(side, n_steps, per_b):
    specs, shapes = [], []
    for w in side:
        rows, cols = w.shape
        blk = next(b for b in range(16, rows + 1, 16) if rows % b == 0 and rows // b <= n_steps)
        last = rows // blk - 1
        specs.append(pl.BlockSpec(
            (blk, cols), lambda b, i, last=last: (jnp.minimum(b * per_b + i, last), 0)))
        shapes.append(jax.ShapeDtypeStruct(w.shape, jnp.bfloat16))
    return specs, shapes


def _attn_call(q, k_lat, k_ctx, v_lat, v_ctx, side, tq):
    bsz, _, seq, _ = q.shape
    assert seq % tq == 0 and tq % Q_SUB == 0, (seq, tq)
    n_ctx = k_ctx.shape[2]
    half = MLA_HEADS * V_HEAD
    side_specs, side_shapes = _side_cast_specs(side, bsz * (seq // tq), seq // tq)
    return pl.pallas_call(
        _attn_kernel,
        grid=(bsz, seq // tq),
        in_specs=[pl.BlockSpec((1, MLA_HEADS, tq, HEAD_PAD), lambda b, i: (b, 0, i, 0)),
                  pl.BlockSpec((1, MLA_HEADS, seq, HEAD_PAD), lambda b, i: (b, 0, 0, 0)),
                  pl.BlockSpec((1, MLA_HEADS, n_ctx, HEAD_PAD), lambda b, i: (b, 0, 0, 0)),
                  pl.BlockSpec((1, seq, MLA_HEADS * LANES), lambda b, i: (b, 0, 0)),
                  pl.BlockSpec((1, n_ctx, MLA_HEADS * LANES), lambda b, i: (b, 0, 0))] + side_specs,
        out_specs=[pl.BlockSpec((1, tq, half), lambda b, i: (b, i, 0))] + side_specs,
        out_shape=[jax.ShapeDtypeStruct((bsz, seq, half), jnp.bfloat16)] + side_shapes,
        compiler_params=pltpu.CompilerParams(
            dimension_semantics=("arbitrary", "arbitrary"),
            vmem_limit_bytes=VMEM_LIMIT),
        name="attention",
    )(q, k_lat, k_ctx, v_lat, v_ctx, *side)


def _out_ffn_kernel(x_ref, attn_ref, sg_ref, modm_ref, wout_ref, mod_ref, nw_ref,
                    w1_ref, w3_ref, w2_ref, o_ref, h_ref, acc_ref):
    y = (jnp.dot(attn_ref[0], wout_ref[0], preferred_element_type=jnp.float32)
         + jnp.dot(sg_ref[0], wout_ref[1], preferred_element_type=jnp.float32))
    x = x_ref[0] + modm_ref[0, 2] * y
    o_ref[0] = _ffn_core(x, mod_ref, nw_ref, w1_ref, w3_ref, w2_ref, h_ref, acc_ref)


def _out_ffn_call(x, attn, sg, mod, w_out, norm_w, w1, w3, w2, tm):
    bsz, seq, _ = x.shape
    assert seq % tm == 0, (seq, tm)
    half = MLA_HEADS * V_HEAD
    row = lambda w: pl.BlockSpec((1, tm, w), lambda b, i: (b, i, 0))
    return pl.pallas_call(
        _out_ffn_kernel,
        grid=(bsz, seq // tm),
        in_specs=[row(D_MODEL), row(half), row(GMLP_WIDTH), _mod_spec(1, True),
                  _const_spec((2, half, D_MODEL)), _mod_spec(2, True), _const_spec((1, D_MODEL)),
                  _const_spec((D_MODEL, D_FF)), _const_spec((D_MODEL, D_FF)),
                  _const_spec((N_FF, FF_TILE, D_MODEL))],
        out_specs=row(D_MODEL),
        out_shape=jax.ShapeDtypeStruct(x.shape, jnp.float32),
        scratch_shapes=[pltpu.VMEM((tm, D_MODEL), jnp.bfloat16),
                        pltpu.VMEM((tm, D_MODEL), jnp.float32)],
        compiler_params=pltpu.CompilerParams(dimension_semantics=("arbitrary", "arbitrary"),
                                             vmem_limit_bytes=VMEM_LIMIT),
        name="out_ffn",
    )(x, attn, sg, mod, w_out, mod, norm_w, w1, w3, w2)


def _ffn_weights(w1, w3, w2):
    bf = jnp.bfloat16
    return w1.astype(bf), w3.astype(bf), w2.astype(bf).reshape(N_FF, FF_TILE, D_MODEL)


def _rot_cols(w, start, signed=True):
    half = AXIS_DIM // 2
    parts = []
    for blk in range(QK_ROPE // half):
        src = start + (blk + 1) * half if blk % 2 == 0 else start + (blk - 1) * half
        piece = w[..., src:src + half]
        parts.append(-piece if (signed and blk % 2 == 0) else piece)
    return jnp.concatenate(parts, axis=-1)


def _selectors():
    r = np.arange(2 * LANES)[:, None]
    c = np.arange(2 * LANES)[None, :]
    sel_k = ((r < 64) & (c < LANES)) | ((r >= 64) & (r < LANES) & (c >= LANES)) | (
        (r >= LANES) & (r < LANES + QK_ROPE))
    ro = r - LANES
    sel_q = ((r < QK_HEAD) & (c < LANES)) | (
        (r >= LANES) & ((ro < QK_ROPE) | (ro >= 2 * QK_ROPE)) & (c >= LANES))
    sel_v = (r // GMLP_GROUP_DIM) == (c // GMLP_GROUP_DIM)
    return [jnp.asarray(m, jnp.bfloat16) for m in (sel_k, sel_q, sel_v)]


def _mix_weights(norm2_w, w_in, q_a_norm_w, w_uq, kv_a_norm_w, w_ukv, q_norm_w, k_norm_w,
                 v_norm_w, w_s, b_s):
    bf = jnp.bfloat16
    kpe = w_in[:, KV_LORA:KV_COLS]
    kpe_rot = _rot_cols(w_in, KV_LORA)
    w_in_ext = jnp.concatenate([w_in[:, :KV_LORA], kpe, kpe, kpe_rot, kpe_rot,
                                w_in[:, Q_START:]], axis=1).astype(bf)
    ukv = w_ukv.reshape(KV_LORA, MLA_HEADS, QK_NOPE + V_HEAD)
    w_ukv_p = jnp.concatenate([ukv[:, :, :QK_NOPE].reshape(KV_LORA, -1),
                               ukv[:, :, QK_NOPE:].reshape(KV_LORA, -1)], axis=1).astype(bf)
    cols = []
    for h in range(MLA_HEADS):
        base = h * QK_HEAD
        nope = w_uq[:, base:base + QK_NOPE]
        pes = w_uq[:, base + QK_NOPE:base + QK_HEAD]
        rots = _rot_cols(w_uq, base + QK_NOPE)
        cols += [nope, pes, rots] if h % 2 == 0 else [pes, rots, nope]
    w_uq_ext = jnp.concatenate(cols, axis=1).astype(bf)
    qn, qp, qr = q_norm_w[:QK_NOPE], q_norm_w[QK_NOPE:], _rot_cols(q_norm_w, QK_NOPE, signed=False)
    wq = jnp.concatenate([qn, qp, qr, qp, qr, qn])[None]
    kn, kp, kr = k_norm_w[:QK_NOPE], k_norm_w[QK_NOPE:], _rot_cols(k_norm_w, QK_NOPE, signed=False)
    sel_k, sel_q, sel_v = _selectors()
    return dict(
        sel_k=sel_k, sel_q=sel_q, sel_v=sel_v,
        norm2=norm2_w[None], w_in=w_in_ext, kvn=kv_a_norm_w[None], w_ukv=w_ukv_p,
        wk_nope=jnp.concatenate([kn, kn])[None],
        wk_ab=jnp.concatenate([kp, kp, kr, kr])[None],
        qan=q_a_norm_w[None], w_uq=w_uq_ext, wq=wq,
        vnw=v_norm_w.reshape(1, GMLP_WIDTH),
        ws=w_s.astype(bf).reshape(GMLP_GROUPS // 2, 2, CHUNK, CHUNK).transpose(0, 2, 1, 3
                                   ).reshape(GMLP_GROUPS // 2, CHUNK, 2 * CHUNK),
        bs=jnp.broadcast_to(b_s.T[:, :, None], (CHUNK, GMLP_GROUPS, GMLP_GROUP_DIM)
                            ).reshape(CHUNK, GMLP_WIDTH),
    )


def _rope_tables(seq, n_ctx):
    f32 = jnp.float32
    rows_n = seq // GRID_W
    rows = jnp.repeat(jnp.arange(rows_n, dtype=f32), GRID_W)
    cols = jnp.tile(jnp.arange(GRID_W, dtype=f32), rows_n)
    inv = ROPE_BASE ** (-jnp.arange(0, AXIS_DIM, 2, dtype=f32) / AXIS_DIM)
    ang_r = rows[:, None] * inv
    ang_c = cols[:, None] * inv
    ang = jnp.concatenate([ang_r, ang_r, ang_c, ang_c], axis=-1)
    cos, sin = jnp.cos(ang), jnp.sin(ang)
    one = jnp.ones((seq, 64), f32)
    lat = dict(k=jnp.concatenate([cos, cos, sin, sin], 1),
               q=jnp.concatenate([one, cos, sin, cos, sin, one], 1))
    ctx = dict(k=jnp.concatenate([jnp.ones((n_ctx, 64), f32), jnp.zeros((n_ctx, 64), f32)], 1))
    return lat, ctx


def kernel(x, c, ctx, c_ctx, w_ada, b_ada, norm1_w, ffn1_w1, ffn1_w3, ffn1_w2, norm2_w, w_in,
           q_a_norm_w, w_uq, kv_a_norm_w, w_ukv, q_norm_w, k_norm_w, v_norm_w, w_s, b_s, w_out,
           norm3_w, ffn2_w1, ffn2_w3, ffn2_w2):
    bsz, seq, _ = x.shape
    n_ctx = ctx.shape[1]
    rows = -(-(bsz + 1) // 8) * 8
    cc = jnp.concatenate([c, c_ctx[None], jnp.zeros((rows - bsz - 1, D_MODEL), jnp.float32)], 0)
    mod = _ada_call(cc, w_ada[0], b_ada[0][None]).reshape(rows, N_MOD, 1, D_MODEL)
    mod_ctx = mod[bsz:bsz + 1]

    f1 = _ffn_weights(ffn1_w1[0], ffn1_w3[0], ffn1_w2[0])
    side = (ffn2_w1[0], ffn2_w3[0], ffn2_w2[0], w_out[0])
    wts = _mix_weights(norm2_w[0], w_in[0], q_a_norm_w[0], w_uq[0], kv_a_norm_w[0], w_ukv[0],
                       q_norm_w[0], k_norm_w[0], v_norm_w[0], w_s[0], b_s[0])
    tabs_lat, tabs_ctx = _rope_tables(seq, n_ctx)

    x1 = _ffn_call(x, mod, 0, True, norm1_w, *f1, tm=ROW_TILE)
    ctx1 = _ffn_call(ctx.reshape(1, bsz * n_ctx, D_MODEL), mod_ctx, 0, False, norm1_w, *f1,
                     tm=ROW_TILE).reshape(bsz, n_ctx, D_MODEL)

    def mix(op_dtype):
        k_lat, v_lat, q, sg = _prep_call(x1, mod, True, wts, tabs_lat, PREP_TILE, True, op_dtype)
        k_ctx, v_ctx = _prep_call(ctx1, mod_ctx, False, wts, tabs_ctx, n_ctx, False, op_dtype)
        return (*_attn_call(q, k_lat, k_ctx, v_lat, v_ctx, side, tq=Q_TILE), sg)

    logit_bound = (QK_HEAD ** 0.5) * jnp.max(jnp.abs(q_norm_w[0])) * jnp.max(jnp.abs(k_norm_w[0]))
    attn, w1b, w3b, w2b, wob, sg = lax.cond(logit_bound <= F8_LOGIT_BOUND,
                                            lambda: mix(F8), lambda: mix(jnp.bfloat16))
    w_out_r = wob.reshape(2, MLA_HEADS * V_HEAD, D_MODEL)
    return _out_ffn_call(x1, attn, sg, mod, w_out_r, norm3_w, w1b, w3b,
                         w2b.reshape(N_FF, FF_TILE, D_MODEL), tm=ROW_TILE)
```

```python
import numpy as np
import jax
import jax.numpy as jnp
from jax import lax
from jax.experimental import pallas as pl
from jax.experimental.pallas import tpu as pltpu

D_MODEL = 1024
GRID_W = 64
MLA_HEADS = 8
QK_NOPE = 64
QK_ROPE = 32
QK_HEAD = QK_NOPE + QK_ROPE
V_HEAD = 64
Q_LORA = 256
KV_LORA = 128
AXIS_DIM = QK_ROPE // 2
ROPE_BASE = 10000.0
GMLP_GROUPS = 8
GMLP_GROUP_DIM = 64
GMLP_WIDTH = GMLP_GROUPS * GMLP_GROUP_DIM
CHUNK = 128
KV_COLS = KV_LORA + QK_ROPE
Q_START = KV_COLS
U_START = KV_COLS + Q_LORA
V_START = U_START + GMLP_WIDTH
IN_COLS = V_START + GMLP_WIDTH
D_FF = 2816
N_MOD = 9
EPS = 1e-6

F8 = jnp.float8_e4m3fn
F8_MAX = 448.0
P_SHIFT = 7.0
F8_LOGIT_BOUND = 16.0
LANES = 128
HEAD_PAD = LANES
N_PAIRS = MLA_HEADS // 2
VMEM_LIMIT = 56 * 1024 * 1024

EXT_KV = 0
EXT_AB = 128
EXT_Q = 256
EXT_U = EXT_Q + Q_LORA
EXT_V = EXT_U + GMLP_WIDTH
EXT_COLS = EXT_V + GMLP_WIDTH
EXT_KV_ONLY = EXT_Q

FF_TILE = 256
N_FF = D_FF // FF_TILE
ROW_TILE = 1024
PREP_TILE = 512
Q_TILE = 1024
Q_SUB = 512


def _rms_scale(x, n):
    return lax.rsqrt(jnp.sum(x * x, axis=-1, keepdims=True) * (1.0 / n) + EPS)


def _silu(a):
    return a / (1.0 + jnp.exp(-a))


def _gelu_tanh(x):
    c = np.float32(np.sqrt(2.0 / np.pi))
    t = jnp.tanh(x * (c + np.float32(c * 0.044715) * (x * x)))
    return x * (0.5 + 0.5 * t)


def _ada_kernel(c_ref, w_ref, b_ref, o_ref):
    s = _silu(c_ref[...]).astype(jnp.bfloat16)
    o_ref[...] = jnp.dot(s, w_ref[...].astype(jnp.bfloat16),
                         preferred_element_type=jnp.float32) + b_ref[...]


def _ada_call(cc, w_ada, b_ada):
    rows = cc.shape[0]
    n = w_ada.shape[1]
    tn = 1024
    return pl.pallas_call(
        _ada_kernel,
        grid=(n // tn,),
        in_specs=[pl.BlockSpec((rows, D_MODEL), lambda j: (0, 0)),
                  pl.BlockSpec((D_MODEL, tn), lambda j: (0, j)),
                  pl.BlockSpec((1, tn), lambda j: (0, j))],
        out_specs=pl.BlockSpec((rows, tn), lambda j: (0, j)),
        out_shape=jax.ShapeDtypeStruct((rows, n), jnp.float32),
        compiler_params=pltpu.CompilerParams(dimension_semantics=("arbitrary",),
                                             vmem_limit_bytes=VMEM_LIMIT),
        name="adaln",
    )(cc, w_ada, b_ada)


def _ffn_core(x, mod_ref, nw_ref, w1_ref, w3_ref, w2_ref, h_ref, acc_ref):
    shift, scale, gate = mod_ref[0, 0], mod_ref[0, 1], mod_ref[0, 2]
    h_ref[...] = (x * _rms_scale(x, D_MODEL) * (nw_ref[...] * (1.0 + scale)) + shift
                  ).astype(jnp.bfloat16)
    acc_ref[...] = jnp.zeros_like(acc_ref)

    for j in range(N_FF):
        hb = h_ref[...]
        cols = slice(j * FF_TILE, (j + 1) * FF_TILE)
        a = jnp.dot(hb, w1_ref[:, cols], preferred_element_type=jnp.float32)
        b = jnp.dot(hb, w3_ref[:, cols], preferred_element_type=jnp.float32)
        g = (_silu(a) * b).astype(jnp.bfloat16)
        acc_ref[...] += jnp.dot(g, w2_ref[j], preferred_element_type=jnp.float32)
    return x + (0.5 * gate) * acc_ref[...]


def _ffn_kernel(x_ref, mod_ref, nw_ref, w1_ref, w3_ref, w2_ref, *rest):
    n_side = (len(rest) - 3) // 2
    o_ref, (h_ref, acc_ref) = rest[n_side], rest[-2:]
    for src, dst in zip(rest[:n_side], rest[n_side + 1:2 * n_side + 1]):
        dst[...] = src[...].astype(dst.dtype)
    o_ref[0] = _ffn_core(x_ref[0], mod_ref, nw_ref, w1_ref, w3_ref, w2_ref, h_ref, acc_ref)


def _side_cast_specs(side, n_steps, per_b):
    specs, shapes = [], []
    for w in side:
        rows, cols = w.shape
        blk = next(b for b in range(16, rows + 1, 16) if rows % b == 0 and rows // b <= n_steps)
        last = rows // blk - 1
        specs.append(pl.BlockSpec(
            (blk, cols), lambda b, i, last=last: (jnp.minimum(b * per_b + i, last), 0)))
        shapes.append(jax.ShapeDtypeStruct(w.shape, jnp.bfloat16))
    return specs, shapes


def _const_spec(shape):
    nd = len(shape)
    return pl.BlockSpec(shape, lambda *_: (0,) * nd, pipeline_mode=pl.Buffered(1))


def _mod_spec(mod_block, per_batch):
    if per_batch:
        return pl.BlockSpec((1, 3, 1, D_MODEL), lambda b, i: (b, mod_block, 0, 0))
    return pl.BlockSpec((1, 3, 1, D_MODEL), lambda b, i: (0, mod_block, 0, 0))


def _ffn_call(x, mod, mod_block, per_batch, norm_w, w1, w3, w2, tm, side=()):
    bsz, seq, _ = x.shape
    assert seq % tm == 0, (seq, tm)
    side_specs, side_shapes = _side_cast_specs(side, bsz * (seq // tm), seq // tm)
    return pl.pallas_call(
        _ffn_kernel,
        grid=(bsz, seq // tm),
        in_specs=[pl.BlockSpec((1, tm, D_MODEL), lambda b, i: (b, i, 0)),
                  _mod_spec(mod_block, per_batch),
                  _const_spec((1, D_MODEL)),
                  _const_spec((D_MODEL, D_FF)),
                  _const_spec((D_MODEL, D_FF)),
                  _const_spec((N_FF, FF_TILE, D_MODEL))] + side_specs,
        out_specs=[pl.BlockSpec((1, tm, D_MODEL), lambda b, i: (b, i, 0))] + side_specs,
        out_shape=[jax.ShapeDtypeStruct(x.shape, jnp.float32)] + side_shapes,
        scratch_shapes=[pltpu.VMEM((tm, D_MODEL), jnp.bfloat16),
                        pltpu.VMEM((tm, D_MODEL), jnp.float32)],
        compiler_params=pltpu.CompilerParams(dimension_semantics=("arbitrary", "arbitrary"),
                                             vmem_limit_bytes=VMEM_LIMIT),
        name="ffn",
    )(x, mod, norm_w, w1, w3, w2, *side)


def _lane_iota(shape):
    return lax.broadcasted_iota(jnp.int32, shape, len(shape) - 1)


def _sq_bf16(x):
    return (x * x).astype(jnp.bfloat16)


def _seg_rms(sq, sel_ref, n):
    ss = jnp.dot(sq, sel_ref[...], preferred_element_type=jnp.float32)
    return lax.rsqrt(ss + n * EPS)


def _clamp(x, bound, enabled):
    return jnp.clip(x, -bound, bound) if enabled else x


def _kv_prep(proj, kvn_ref, wukv_ref, wkn_ref, wkab_ref, tk_ref, selk_ref, k_ref, v_ref):
    tm = proj.shape[0]
    ckv = proj[:, EXT_KV:EXT_KV + KV_LORA]
    ckv = (ckv * _rms_scale(ckv, KV_LORA) * kvn_ref[...]).astype(jnp.bfloat16)
    kv = jnp.dot(ckv, wukv_ref[...], preferred_element_type=jnp.float32)
    lane = _lane_iota((tm, LANES))
    lo = lane < 64
    ab = proj[:, EXT_AB:EXT_AB + LANES]
    root_n = np.float32(np.sqrt(QK_HEAD))
    f8 = k_ref.dtype == F8
    g_rope = _clamp(wkab_ref[...] * root_n, 0.5 * F8_MAX, f8)
    g_nope = _clamp(wkn_ref[...] * root_n, F8_MAX, f8)
    t = ab * (g_rope * tk_ref[...])
    rope = t + pltpu.roll(t, 64, axis=1)
    ab_sq = _sq_bf16(ab)
    one_e = jnp.where(lane == V_HEAD, 1.0, 0.0)
    one_o = jnp.where(lane == 0, 1.0, 0.0)
    for p in range(N_PAIRS):
        vp = kv[:, MLA_HEADS * QK_NOPE + p * LANES:MLA_HEADS * QK_NOPE + (p + 1) * LANES]
        vp = _clamp(vp, F8_MAX, f8)
        v_ref[0, :, (2 * p) * LANES:(2 * p + 1) * LANES] = jnp.where(lo, vp, one_e).astype(v_ref.dtype)
        v_ref[0, :, (2 * p + 1) * LANES:(2 * p + 2) * LANES] = jnp.where(lo, one_o, vp).astype(v_ref.dtype)
        kp = kv[:, p * LANES:(p + 1) * LANES]
        r = _seg_rms(jnp.concatenate([_sq_bf16(kp), ab_sq], axis=1), selk_ref, QK_HEAD)
        kw = kp * g_nope
        k_ref[0, 2 * p] = (jnp.where(lo, kw, rope) * r[:, :LANES]).astype(k_ref.dtype)
        k_ref[0, 2 * p + 1] = (jnp.where(lo, rope, kw) * r[:, LANES:]).astype(k_ref.dtype)


def _modulated_proj(x, mod_ref, nw_ref, win_ref):
    shift, scale = mod_ref[0, 0], mod_ref[0, 1]
    h = (x * _rms_scale(x, D_MODEL) * (nw_ref[...] * (1.0 + scale)) + shift).astype(jnp.bfloat16)
    return jnp.dot(h, win_ref[...], preferred_element_type=jnp.float32)


def _prep_kernel(x_ref, mod_ref, nw_ref, win_ref, kvn_ref, wukv_ref, wkn_ref, wkab_ref, tk_ref,
                 selk_ref, qan_ref, wuq_ref, wqn_ref, tq_ref, selq_ref, vnw_ref, selv_ref,
                 ws_ref, bs_ref, k_ref, v_ref, q_ref, sg_ref):
    tm = x_ref.shape[1]
    proj = _modulated_proj(x_ref[0], mod_ref, nw_ref, win_ref)
    _kv_prep(proj, kvn_ref, wukv_ref, wkn_ref, wkab_ref, tk_ref, selk_ref, k_ref, v_ref)

    cq = proj[:, EXT_Q:EXT_Q + Q_LORA]
    cq = (cq * _rms_scale(cq, Q_LORA) * qan_ref[...]).astype(jnp.bfloat16)
    qall = jnp.dot(cq, wuq_ref[...], preferred_element_type=jnp.float32)
    lane = _lane_iota((tm, LANES))
    gq = _clamp(wqn_ref[...] * np.float32(np.log2(np.e)), F8_MAX, q_ref.dtype == F8)
    tabs = [gq[:, par * LANES:(par + 1) * LANES] * tq_ref[:, par * LANES:(par + 1) * LANES]
            for par in range(2)]
    for p in range(N_PAIRS):
        qp = qall[:, 2 * p * LANES:(2 * p + 2) * LANES]
        r = _seg_rms(_sq_bf16(qp), selq_ref, QK_HEAD)
        for par in range(2):
            hl = slice(par * LANES, (par + 1) * LANES)
            q_ref[0, 2 * p + par] = (qp[:, hl] * r[:, hl] * tabs[par]).astype(q_ref.dtype)

    u = _gelu_tanh(proj[:, EXT_U:EXT_U + GMLP_WIDTH])
    v = _gelu_tanh(proj[:, EXT_V:EXT_V + GMLP_WIDTH])
    lo = lane < 64
    vn_tiles = []
    for p in range(GMLP_GROUPS // 2):
        if p % 2 == 0:
            v2 = v[:, p * LANES:(p + 2) * LANES]
            vn2 = (v2 * _seg_rms(_sq_bf16(v2), selv_ref, GMLP_GROUP_DIM)
                   * (vnw_ref[:, p * LANES:(p + 2) * LANES] * np.float32(np.sqrt(GMLP_GROUP_DIM))))
        vn = vn2[:, (p % 2) * LANES:(p % 2 + 1) * LANES]
        vn_tiles.append((jnp.where(lo, vn, 0.0).astype(jnp.bfloat16),
                         jnp.where(lo, 0.0, vn).astype(jnp.bfloat16)))
    for c in range(tm // CHUNK):
        rows = slice(c * CHUNK, (c + 1) * CHUNK)
        for p in range(GMLP_GROUPS // 2):
            vblk = jnp.concatenate([vn_tiles[p][0][rows], vn_tiles[p][1][rows]], axis=0)
            s = (jnp.dot(ws_ref[p], vblk, preferred_element_type=jnp.float32)
                 + bs_ref[:, p * LANES:(p + 1) * LANES])
            sg_ref[0, rows, p * LANES:(p + 1) * LANES] = (
                u[rows, p * LANES:(p + 1) * LANES] * s).astype(jnp.bfloat16)


def _kvonly_kernel(x_ref, mod_ref, nw_ref, win_ref, kvn_ref, wukv_ref, wkn_ref, wkab_ref, tk_ref,
                   selk_ref, k_ref, v_ref):
    proj = _modulated_proj(x_ref[0], mod_ref, nw_ref, win_ref)
    _kv_prep(proj, kvn_ref, wukv_ref, wkn_ref, wkab_ref, tk_ref, selk_ref, k_ref, v_ref)


def _prep_call(x, mod, per_batch, wts, tabs, tm, with_q, op_dtype):
    bsz, seq, _ = x.shape
    assert seq % tm == 0 and tm % CHUNK == 0, (seq, tm)
    x_spec = pl.BlockSpec((1, tm, D_MODEL), lambda b, i: (b, i, 0))
    tab_spec = lambda w: pl.BlockSpec((tm, w), lambda b, i: (i, 0))
    ncols = EXT_COLS if with_q else EXT_KV_ONLY
    kv_specs = [x_spec, _mod_spec(1, per_batch), _const_spec((1, D_MODEL)),
                _const_spec((D_MODEL, ncols)), _const_spec((1, KV_LORA)),
                _const_spec((KV_LORA, MLA_HEADS * LANES)), _const_spec((1, LANES)),
                _const_spec((1, LANES)), tab_spec(LANES), _const_spec((2 * LANES, 2 * LANES))]
    kv_args = [x, mod, wts["norm2"], wts["w_in"] if with_q else wts["w_in"][:, :EXT_KV_ONLY],
               wts["kvn"], wts["w_ukv"], wts["wk_nope"], wts["wk_ab"], tabs["k"], wts["sel_k"]]
    k_shape = jax.ShapeDtypeStruct((bsz, MLA_HEADS, seq, HEAD_PAD), op_dtype)
    v_shape = jax.ShapeDtypeStruct((bsz, seq, MLA_HEADS * LANES), op_dtype)
    sg_shape = jax.ShapeDtypeStruct((bsz, seq, GMLP_WIDTH), jnp.bfloat16)
    k_spec = pl.BlockSpec((1, MLA_HEADS, tm, HEAD_PAD), lambda b, i: (b, 0, i, 0))
    v_spec = pl.BlockSpec((1, tm, MLA_HEADS * LANES), lambda b, i: (b, i, 0))
    sg_spec = pl.BlockSpec((1, tm, GMLP_WIDTH), lambda b, i: (b, i, 0))
    params = pltpu.CompilerParams(dimension_semantics=("arbitrary", "arbitrary"),
                                  vmem_limit_bytes=VMEM_LIMIT)
    if not with_q:
        return pl.pallas_call(
            _kvonly_kernel, grid=(bsz, seq // tm), in_specs=kv_specs,
            out_specs=[k_spec, v_spec], out_shape=[k_shape, v_shape],
            compiler_params=params, name="kv_prep")(*kv_args)
    q_specs = [_const_spec((1, Q_LORA)), _const_spec((Q_LORA, MLA_HEADS * LANES)),
               _const_spec((1, 2 * LANES)), tab_spec(2 * LANES), _const_spec((2 * LANES, 2 * LANES)),
               _const_spec((1, GMLP_WIDTH)), _const_spec((2 * LANES, 2 * LANES)),
               _const_spec((GMLP_GROUPS // 2, CHUNK, 2 * CHUNK)), _const_spec((CHUNK, GMLP_WIDTH))]
    q_args = [wts["qan"], wts["w_uq"], wts["wq"], tabs["q"], wts["sel_q"], wts["vnw"], wts["sel_v"],
              wts["ws"], wts["bs"]]
    return pl.pallas_call(
        _prep_kernel, grid=(bsz, seq // tm), in_specs=kv_specs + q_specs,
        out_specs=[k_spec, v_spec, k_spec, sg_spec],
        out_shape=[k_shape, v_shape, k_shape, sg_shape],
        compiler_params=params, name="mix_prep")(*kv_args, *q_args)


def _attn_kernel(q_ref, kl_ref, kc_ref, vl_ref, vc_ref, o_ref):
    for r0 in range(0, q_ref.shape[2], Q_SUB):
        _attn_rows(slice(r0, r0 + Q_SUB), q_ref, kl_ref, kc_ref, vl_ref, vc_ref, o_ref)


def _attn_rows(rows, q_ref, kl_ref, kc_ref, vl_ref, vc_ref, o_ref):
    nt = (((1,), (1,)), ((), ()))
    lane = _lane_iota((Q_SUB, LANES))
    lo = lane < 64
    ones_lane = (lane == V_HEAD, lane == 0)
    op_dtype = vl_ref.dtype
    p_shift = P_SHIFT if op_dtype == F8 else 0.0
    outs = []
    for h in range(MLA_HEADS):
        q = q_ref[0, h, rows]
        par = h % 2
        pv = slice((h - par) * LANES, (h - par + 2) * LANES)
        s1 = lax.dot_general(q, kl_ref[0, h], nt, preferred_element_type=jnp.float32)
        s2 = lax.dot_general(q, kc_ref[0, h], nt, preferred_element_type=jnp.float32)
        m = jnp.maximum(jnp.max(s1, axis=-1, keepdims=True),
                        jnp.max(s2, axis=-1, keepdims=True)) - p_shift
        p1 = jnp.exp2(s1 - m).astype(op_dtype)
        p2 = jnp.exp2(s2 - m).astype(op_dtype)
        o = (jnp.dot(p1, vl_ref[0, :, pv], preferred_element_type=jnp.float32)
             + jnp.dot(p2, vc_ref[0, :, pv], preferred_element_type=jnp.float32))
        o = o[:, par * LANES:(par + 1) * LANES]
        l = jnp.sum(jnp.where(ones_lane[par], o, 0.0), axis=-1, keepdims=True)
        outs.append(o / l)
    for p in range(N_PAIRS):
        o_ref[0, rows, p * LANES:(p + 1) * LANES] = jnp.where(
            lo, outs[2 * p], outs[2 * p + 1]).astype(jnp.bfloat16)


def _attn_call(q, k_lat, k_ctx, v_lat, v_ctx, tq):
    bsz, _, seq, _ = q.shape
    assert seq % tq == 0 and tq % Q_SUB == 0, (seq, tq)
    n_ctx = k_ctx.shape[2]
    half = MLA_HEADS * V_HEAD
    return pl.pallas_call(
        _attn_kernel,
        grid=(bsz, seq // tq),
        in_specs=[pl.BlockSpec((1, MLA_HEADS, tq, HEAD_PAD), lambda b, i: (b, 0, i, 0)),
                  pl.BlockSpec((1, MLA_HEADS, seq, HEAD_PAD), lambda b, i: (b, 0, 0, 0)),
                  pl.BlockSpec((1, MLA_HEADS, n_ctx, HEAD_PAD), lambda b, i: (b, 0, 0, 0)),
                  pl.BlockSpec((1, seq, MLA_HEADS * LANES), lambda b, i: (b, 0, 0)),
                  pl.BlockSpec((1, n_ctx, MLA_HEADS * LANES), lambda b, i: (b, 0, 0))],
        out_specs=pl.BlockSpec((1, tq, half), lambda b, i: (b, i, 0)),
        out_shape=jax.ShapeDtypeStruct((bsz, seq, half), jnp.bfloat16),
        compiler_params=pltpu.CompilerParams(
            dimension_semantics=("arbitrary", "arbitrary"),
            vmem_limit_bytes=VMEM_LIMIT),
        name="attention",
    )(q, k_lat, k_ctx, v_lat, v_ctx)


def _out_ffn_kernel(x_ref, attn_ref, sg_ref, modm_ref, wout_ref, mod_ref, nw_ref,
                    w1_ref, w3_ref, w2_ref, o_ref, h_ref, acc_ref):
    y = (jnp.dot(attn_ref[0], wout_ref[0], preferred_element_type=jnp.float32)
         + jnp.dot(sg_ref[0], wout_ref[1], preferred_element_type=jnp.float32))
    x = x_ref[0] + modm_ref[0, 2] * y
    o_ref[0] = _ffn_core(x, mod_ref, nw_ref, w1_ref, w3_ref, w2_ref, h_ref, acc_ref)


def _out_ffn_call(x, attn, sg, mod, w_out, norm_w, w1, w3, w2, tm):
    bsz, seq, _ = x.shape
    assert seq % tm == 0, (seq, tm)
    half = MLA_HEADS * V_HEAD
    row = lambda w: pl.BlockSpec((1, tm, w), lambda b, i: (b, i, 0))
    return pl.pallas_call(
        _out_ffn_kernel,
        grid=(bsz, seq // tm),
        in_specs=[row(D_MODEL), row(half), row(GMLP_WIDTH), _mod_spec(1, True),
                  _const_spec((2, half, D_MODEL)), _mod_spec(2, True), _const_spec((1, D_MODEL)),
                  _const_spec((D_MODEL, D_FF)), _const_spec((D_MODEL, D_FF)),
                  _const_spec((N_FF, FF_TILE, D_MODEL))],
        out_specs=row(D_MODEL),
        out_shape=jax.ShapeDtypeStruct(x.shape, jnp.float32),
        scratch_shapes=[pltpu.VMEM((tm, D_MODEL), jnp.bfloat16),
                        pltpu.VMEM((tm, D_MODEL), jnp.float32)],
        compiler_params=pltpu.CompilerParams(dimension_semantics=("arbitrary", "arbitrary"),
                                             vmem_limit_bytes=VMEM_LIMIT),
        name="out_ffn",
    )(x, attn, sg, mod, w_out, mod, norm_w, w1, w3, w2)


def _ffn_weights(w1, w3, w2):
    bf = jnp.bfloat16
    return w1.astype(bf), w3.astype(bf), w2.astype(bf).reshape(N_FF, FF_TILE, D_MODEL)


def _rot_cols(w, start, signed=True):
    half = AXIS_DIM // 2
    parts = []
    for blk in range(QK_ROPE // half):
        src = start + (blk + 1) * half if blk % 2 == 0 else start + (blk - 1) * half
        piece = w[..., src:src + half]
        parts.append(-piece if (signed and blk % 2 == 0) else piece)
    return jnp.concatenate(parts, axis=-1)


def _selectors():
    r = np.arange(2 * LANES)[:, None]
    c = np.arange(2 * LANES)[None, :]
    sel_k = ((r < 64) & (c < LANES)) | ((r >= 64) & (r < LANES) & (c >= LANES)) | (
        (r >= LANES) & (r < LANES + QK_ROPE))
    ro = r - LANES
    sel_q = ((r < QK_HEAD) & (c < LANES)) | (
        (r >= LANES) & ((ro < QK_ROPE) | (ro >= 2 * QK_ROPE)) & (c >= LANES))
    sel_v = (r // GMLP_GROUP_DIM) == (c // GMLP_GROUP_DIM)
    return [jnp.asarray(m, jnp.bfloat16) for m in (sel_k, sel_q, sel_v)]


def _mix_weights(norm2_w, w_in, q_a_norm_w, w_uq, kv_a_norm_w, w_ukv, q_norm_w, k_norm_w,
                 v_norm_w, w_s, b_s):
    bf = jnp.bfloat16
    kpe = w_in[:, KV_LORA:KV_COLS]
    kpe_rot = _rot_cols(w_in, KV_LORA)
    w_in_ext = jnp.concatenate([w_in[:, :KV_LORA], kpe, kpe, kpe_rot, kpe_rot,
                                w_in[:, Q_START:]], axis=1).astype(bf)
    ukv = w_ukv.reshape(KV_LORA, MLA_HEADS, QK_NOPE + V_HEAD)
    w_ukv_p = jnp.concatenate([ukv[:, :, :QK_NOPE].reshape(KV_LORA, -1),
                               ukv[:, :, QK_NOPE:].reshape(KV_LORA, -1)], axis=1).astype(bf)
    cols = []
    for h in range(MLA_HEADS):
        base = h * QK_HEAD
        nope = w_uq[:, base:base + QK_NOPE]
        pes = w_uq[:, base + QK_NOPE:base + QK_HEAD]
        rots = _rot_cols(w_uq, base + QK_NOPE)
        cols += [nope, pes, rots] if h % 2 == 0 else [pes, rots, nope]
    w_uq_ext = jnp.concatenate(cols, axis=1).astype(bf)
    qn, qp, qr = q_norm_w[:QK_NOPE], q_norm_w[QK_NOPE:], _rot_cols(q_norm_w, QK_NOPE, signed=False)
    wq = jnp.concatenate([qn, qp, qr, qp, qr, qn])[None]
    kn, kp, kr = k_norm_w[:QK_NOPE], k_norm_w[QK_NOPE:], _rot_cols(k_norm_w, QK_NOPE, signed=False)
    sel_k, sel_q, sel_v = _selectors()
    return dict(
        sel_k=sel_k, sel_q=sel_q, sel_v=sel_v,
        norm2=norm2_w[None], w_in=w_in_ext, kvn=kv_a_norm_w[None], w_ukv=w_ukv_p,
        wk_nope=jnp.concatenate([kn, kn])[None],
        wk_ab=jnp.concatenate([kp, kp, kr, kr])[None],
        qan=q_a_norm_w[None], w_uq=w_uq_ext, wq=wq,
        vnw=v_norm_w.reshape(1, GMLP_WIDTH),
        ws=w_s.astype(bf).reshape(GMLP_GROUPS // 2, 2, CHUNK, CHUNK).transpose(0, 2, 1, 3
                                   ).reshape(GMLP_GROUPS // 2, CHUNK, 2 * CHUNK),
        bs=jnp.broadcast_to(b_s.T[:, :, None], (CHUNK, GMLP_GROUPS, GMLP_GROUP_DIM)
                            ).reshape(CHUNK, GMLP_WIDTH),
    )


def _rope_tables(seq, n_ctx):
    f32 = jnp.float32
    rows_n = seq // GRID_W
    rows = jnp.repeat(jnp.arange(rows_n, dtype=f32), GRID_W)
    cols = jnp.tile(jnp.arange(GRID_W, dtype=f32), rows_n)
    inv = ROPE_BASE ** (-jnp.arange(0, AXIS_DIM, 2, dtype=f32) / AXIS_DIM)
    ang_r = rows[:, None] * inv
    ang_c = cols[:, None] * inv
    ang = jnp.concatenate([ang_r, ang_r, ang_c, ang_c], axis=-1)
    cos, sin = jnp.cos(ang), jnp.sin(ang)
    one = jnp.ones((seq, 64), f32)
    lat = dict(k=jnp.concatenate([cos, cos, sin, sin], 1),
               q=jnp.concatenate([one, cos, sin, cos, sin, one], 1))
    ctx = dict(k=jnp.concatenate([jnp.ones((n_ctx, 64), f32), jnp.zeros((n_ctx, 64), f32)], 1))
    return lat, ctx


def kernel(x, c, ctx, c_ctx, w_ada, b_ada, norm1_w, ffn1_w1, ffn1_w3, ffn1_w2, norm2_w, w_in,
           q_a_norm_w, w_uq, kv_a_norm_w, w_ukv, q_norm_w, k_norm_w, v_norm_w, w_s, b_s, w_out,
           norm3_w, ffn2_w1, ffn2_w3, ffn2_w2):
    bsz, seq, _ = x.shape
    n_ctx = ctx.shape[1]
    rows = -(-(bsz + 1) // 8) * 8
    cc = jnp.concatenate([c, c_ctx[None], jnp.zeros((rows - bsz - 1, D_MODEL), jnp.float32)], 0)
    mod = _ada_call(cc, w_ada[0], b_ada[0][None]).reshape(rows, N_MOD, 1, D_MODEL)
    mod_ctx = mod[bsz:bsz + 1]

    f1 = _ffn_weights(ffn1_w1[0], ffn1_w3[0], ffn1_w2[0])
    side = (ffn2_w1[0], ffn2_w3[0], ffn2_w2[0], w_out[0])
    wts = _mix_weights(norm2_w[0], w_in[0], q_a_norm_w[0], w_uq[0], kv_a_norm_w[0], w_ukv[0],
                       q_norm_w[0], k_norm_w[0], v_norm_w[0], w_s[0], b_s[0])
    tabs_lat, tabs_ctx = _rope_tables(seq, n_ctx)

    x1, w1b, w3b, w2b, wob = _ffn_call(x, mod, 0, True, norm1_w, *f1, tm=ROW_TILE, side=side)
    ctx1 = _ffn_call(ctx.reshape(1, bsz * n_ctx, D_MODEL), mod_ctx, 0, False, norm1_w, *f1,
                     tm=ROW_TILE)[0].reshape(bsz, n_ctx, D_MODEL)

    def mix(op_dtype):
        k_lat, v_lat, q, sg = _prep_call(x1, mod, True, wts, tabs_lat, PREP_TILE, True, op_dtype)
        k_ctx, v_ctx = _prep_call(ctx1, mod_ctx, False, wts, tabs_ctx, n_ctx, False, op_dtype)
        return _attn_call(q, k_lat, k_ctx, v_lat, v_ctx, tq=Q_TILE), sg

    logit_bound = (QK_HEAD ** 0.5) * jnp.max(jnp.abs(q_norm_w[0])) * jnp.max(jnp.abs(k_norm_w[0]))
    attn, sg = lax.cond(logit_bound <= F8_LOGIT_BOUND, lambda: mix(F8), lambda: mix(jnp.bfloat16))
    w_out_r = wob.reshape(2, MLA_HEADS * V_HEAD, D_MODEL)
    return _out_ffn_call(x1, attn, sg, mod, w_out_r, norm3_w, w1b, w3b,
                         w2b.reshape(N_FF, FF_TILE, D_MODEL), tm=ROW_TILE)
```

```python
import numpy as np
import jax
import jax.numpy as jnp
from jax import lax
from jax.experimental import pallas as pl
from jax.experimental.pallas import tpu as pltpu

D_MODEL = 1024
GRID_W = 64
MLA_HEADS = 8
QK_NOPE = 64
QK_ROPE = 32
QK_HEAD = QK_NOPE + QK_ROPE
V_HEAD = 64
Q_LORA = 256
KV_LORA = 128
AXIS_DIM = QK_ROPE // 2
ROPE_BASE = 10000.0
GMLP_GROUPS = 8
GMLP_GROUP_DIM = 64
GMLP_WIDTH = GMLP_GROUPS * GMLP_GROUP_DIM
CHUNK = 128
KV_COLS = KV_LORA + QK_ROPE
Q_START = KV_COLS
U_START = KV_COLS + Q_LORA
V_START = U_START + GMLP_WIDTH
IN_COLS = V_START + GMLP_WIDTH
D_FF = 2816
N_MOD = 9
EPS = 1e-6

F8 = jnp.float8_e4m3fn
F8_MAX = 448.0
P_SHIFT = 7.0
F8_LOGIT_BOUND = 16.0
LANES = 128
HEAD_PAD = LANES
N_PAIRS = MLA_HEADS // 2
VMEM_LIMIT = 56 * 1024 * 1024

EXT_KV = 0
EXT_AB = 128
EXT_Q = 256
EXT_U = EXT_Q + Q_LORA
EXT_V = EXT_U + GMLP_WIDTH
EXT_COLS = EXT_V + GMLP_WIDTH
EXT_KV_ONLY = EXT_Q

FF_TILE = 256
N_FF = D_FF // FF_TILE
ROW_TILE = 1024
PREP_TILE = 512
Q_TILE = 1024
Q_SUB = 512


def _rms_scale(x, n):
    return lax.rsqrt(jnp.sum(x * x, axis=-1, keepdims=True) * (1.0 / n) + EPS)


def _silu(a):
    return a / (1.0 + jnp.exp(-a))


def _gelu_tanh(x):
    c = np.float32(np.sqrt(2.0 / np.pi))
    t = jnp.tanh(x * (c + np.float32(c * 0.044715) * (x * x)))
    return x * (0.5 + 0.5 * t)


def _ada_kernel(c_ref, w_ref, b_ref, o_ref):
    s = _silu(c_ref[...]).astype(jnp.bfloat16)
    o_ref[...] = jnp.dot(s, w_ref[...].astype(jnp.bfloat16),
                         preferred_element_type=jnp.float32) + b_ref[...]


def _ada_call(cc, w_ada, b_ada):
    rows = cc.shape[0]
    n = w_ada.shape[1]
    tn = 2304
    assert n % tn == 0, (n, tn)
    return pl.pallas_call(
        _ada_kernel,
        grid=(n // tn,),
        in_specs=[pl.BlockSpec((rows, D_MODEL), lambda j: (0, 0)),
                  pl.BlockSpec((D_MODEL, tn), lambda j: (0, j)),
                  pl.BlockSpec((1, tn), lambda j: (0, j))],
        out_specs=pl.BlockSpec((rows, tn), lambda j: (0, j)),
        out_shape=jax.ShapeDtypeStruct((rows, n), jnp.float32),
        compiler_params=pltpu.CompilerParams(dimension_semantics=("arbitrary",),
                                             vmem_limit_bytes=VMEM_LIMIT),
        name="adaln",
    )(cc, w_ada, b_ada)


def _ffn_core(x, mod_ref, nw_ref, w1_ref, w3_ref, w2_ref, h_ref, acc_ref):
    shift, scale, gate = mod_ref[0, 0], mod_ref[0, 1], mod_ref[0, 2]
    h_ref[...] = (x * _rms_scale(x, D_MODEL) * (nw_ref[...] * (1.0 + scale)) + shift
                  ).astype(jnp.bfloat16)
    acc_ref[...] = jnp.zeros_like(acc_ref)

    for j in range(N_FF):
        hb = h_ref[...]
        cols = slice(j * FF_TILE, (j + 1) * FF_TILE)
        a = jnp.dot(hb, w1_ref[:, cols], preferred_element_type=jnp.float32)
        b = jnp.dot(hb, w3_ref[:, cols], preferred_element_type=jnp.float32)
        g = (_silu(a) * b).astype(jnp.bfloat16)
        acc_ref[...] += jnp.dot(g, w2_ref[j], preferred_element_type=jnp.float32)
    return x + (0.5 * gate) * acc_ref[...]


def _ffn_kernel(x_ref, mod_ref, nw_ref, w1_ref, w3_ref, w2_ref, o_ref, h_ref, acc_ref):
    o_ref[0] = _ffn_core(x_ref[0], mod_ref, nw_ref, w1_ref, w3_ref, w2_ref, h_ref, acc_ref)


def _const_spec(shape):
    nd = len(shape)
    return pl.BlockSpec(shape, lambda *_: (0,) * nd, pipeline_mode=pl.Buffered(1))


def _mod_spec(mod_block, per_batch):
    if per_batch:
        return pl.BlockSpec((1, 3, 1, D_MODEL), lambda b, i: (b, mod_block, 0, 0))
    return pl.BlockSpec((1, 3, 1, D_MODEL), lambda b, i: (0, mod_block, 0, 0))


def _ffn_call(x, mod, mod_block, per_batch, norm_w, w1, w3, w2, tm):
    bsz, seq, _ = x.shape
    assert seq % tm == 0, (seq, tm)
    return pl.pallas_call(
        _ffn_kernel,
        grid=(bsz, seq // tm),
        in_specs=[pl.BlockSpec((1, tm, D_MODEL), lambda b, i: (b, i, 0)),
                  _mod_spec(mod_block, per_batch),
                  _const_spec((1, D_MODEL)),
                  _const_spec((D_MODEL, D_FF)),
                  _const_spec((D_MODEL, D_FF)),
                  _const_spec((N_FF, FF_TILE, D_MODEL))],
        out_specs=pl.BlockSpec((1, tm, D_MODEL), lambda b, i: (b, i, 0)),
        out_shape=jax.ShapeDtypeStruct(x.shape, jnp.float32),
        scratch_shapes=[pltpu.VMEM((tm, D_MODEL), jnp.bfloat16),
                        pltpu.VMEM((tm, D_MODEL), jnp.float32)],
        compiler_params=pltpu.CompilerParams(dimension_semantics=("arbitrary", "arbitrary"),
                                             vmem_limit_bytes=VMEM_LIMIT),
        name="ffn",
    )(x, mod, norm_w, w1, w3, w2)


def _lane_iota(shape):
    return lax.broadcasted_iota(jnp.int32, shape, len(shape) - 1)


def _sq_bf16(x):
    return (x * x).astype(jnp.bfloat16)


def _seg_rms(sq, sel_ref, n):
    ss = jnp.dot(sq, sel_ref[...], preferred_element_type=jnp.float32)
    return lax.rsqrt(ss + n * EPS)


def _clamp(x, bound, enabled):
    return jnp.clip(x, -bound, bound) if enabled else x


def _kv_prep(proj, kvn_ref, wukv_ref, wkn_ref, wkab_ref, tk_ref, selk_ref, k_ref, v_ref):
    tm = proj.shape[0]
    ckv = proj[:, EXT_KV:EXT_KV + KV_LORA]
    ckv = (ckv * _rms_scale(ckv, KV_LORA) * kvn_ref[...]).astype(jnp.bfloat16)
    kv = jnp.dot(ckv, wukv_ref[...], preferred_element_type=jnp.float32)
    lane = _lane_iota((tm, LANES))
    lo = lane < 64
    ab = proj[:, EXT_AB:EXT_AB + LANES]
    root_n = np.float32(np.sqrt(QK_HEAD))
    f8 = k_ref.dtype == F8
    g_rope = _clamp(wkab_ref[...] * root_n, 0.5 * F8_MAX, f8)
    g_nope = _clamp(wkn_ref[...] * root_n, F8_MAX, f8)
    t = ab * (g_rope * tk_ref[...])
    rope = t + pltpu.roll(t, 64, axis=1)
    ab_sq = _sq_bf16(ab)
    one_e = jnp.where(lane == V_HEAD, 1.0, 0.0)
    one_o = jnp.where(lane == 0, 1.0, 0.0)
    for p in range(N_PAIRS):
        vp = kv[:, MLA_HEADS * QK_NOPE + p * LANES:MLA_HEADS * QK_NOPE + (p + 1) * LANES]
        vp = _clamp(vp, F8_MAX, f8)
        v_ref[0, :, (2 * p) * LANES:(2 * p + 1) * LANES] = jnp.where(lo, vp, one_e).astype(v_ref.dtype)
        v_ref[0, :, (2 * p + 1) * LANES:(2 * p + 2) * LANES] = jnp.where(lo, one_o, vp).astype(v_ref.dtype)
        kp = kv[:, p * LANES:(p + 1) * LANES]
        r = _seg_rms(jnp.concatenate([_sq_bf16(kp), ab_sq], axis=1), selk_ref, QK_HEAD)
        kw = kp * g_nope
        k_ref[0, 2 * p] = (jnp.where(lo, kw, rope) * r[:, :LANES]).astype(k_ref.dtype)
        k_ref[0, 2 * p + 1] = (jnp.where(lo, rope, kw) * r[:, LANES:]).astype(k_ref.dtype)


def _modulated_proj(x, mod_ref, nw_ref, win_ref):
    shift, scale = mod_ref[0, 0], mod_ref[0, 1]
    h = (x * _rms_scale(x, D_MODEL) * (nw_ref[...] * (1.0 + scale)) + shift).astype(jnp.bfloat16)
    return jnp.dot(h, win_ref[...], preferred_element_type=jnp.float32)


def _prep_kernel(x_ref, mod_ref, nw_ref, win_ref, kvn_ref, wukv_ref, wkn_ref, wkab_ref, tk_ref,
                 selk_ref, qan_ref, wuq_ref, wqn_ref, tq_ref, selq_ref, vnw_ref, selv_ref,
                 ws_ref, bs_ref, k_ref, v_ref, q_ref, sg_ref):
    tm = x_ref.shape[1]
    proj = _modulated_proj(x_ref[0], mod_ref, nw_ref, win_ref)
    _kv_prep(proj, kvn_ref, wukv_ref, wkn_ref, wkab_ref, tk_ref, selk_ref, k_ref, v_ref)

    cq = proj[:, EXT_Q:EXT_Q + Q_LORA]
    cq = (cq * _rms_scale(cq, Q_LORA) * qan_ref[...]).astype(jnp.bfloat16)
    qall = jnp.dot(cq, wuq_ref[...], preferred_element_type=jnp.float32)
    lane = _lane_iota((tm, LANES))
    gq = _clamp(wqn_ref[...] * np.float32(np.log2(np.e)), F8_MAX, q_ref.dtype == F8)
    tabs = [gq[:, par * LANES:(par + 1) * LANES] * tq_ref[:, par * LANES:(par + 1) * LANES]
            for par in range(2)]
    for p in range(N_PAIRS):
        qp = qall[:, 2 * p * LANES:(2 * p + 2) * LANES]
        r = _seg_rms(_sq_bf16(qp), selq_ref, QK_HEAD)
        for par in range(2):
            hl = slice(par * LANES, (par + 1) * LANES)
            q_ref[0, 2 * p + par] = (qp[:, hl] * r[:, hl] * tabs[par]).astype(q_ref.dtype)

    u = _gelu_tanh(proj[:, EXT_U:EXT_U + GMLP_WIDTH])
    v = _gelu_tanh(proj[:, EXT_V:EXT_V + GMLP_WIDTH])
    lo = lane < 64
    vn_tiles = []
    for p in range(GMLP_GROUPS // 2):
        if p % 2 == 0:
            v2 = v[:, p * LANES:(p + 2) * LANES]
            vn2 = (v2 * _seg_rms(_sq_bf16(v2), selv_ref, GMLP_GROUP_DIM)
                   * (vnw_ref[:, p * LANES:(p + 2) * LANES] * np.float32(np.sqrt(GMLP_GROUP_DIM))))
        vn = vn2[:, (p % 2) * LANES:(p % 2 + 1) * LANES]
        vn_tiles.append((jnp.where(lo, vn, 0.0).astype(jnp.bfloat16),
                         jnp.where(lo, 0.0, vn).astype(jnp.bfloat16)))
    for c in range(tm // CHUNK):
        rows = slice(c * CHUNK, (c + 1) * CHUNK)
        for p in range(GMLP_GROUPS // 2):
            vblk = jnp.concatenate([vn_tiles[p][0][rows], vn_tiles[p][1][rows]], axis=0)
            s = (jnp.dot(ws_ref[p], vblk, preferred_element_type=jnp.float32)
                 + bs_ref[:, p * LANES:(p + 1) * LANES])
            sg_ref[0, rows, p * LANES:(p + 1) * LANES] = (
                u[rows, p * LANES:(p + 1) * LANES] * s).astype(jnp.bfloat16)


def _kvonly_kernel(x_ref, mod_ref, nw_ref, win_ref, kvn_ref, wukv_ref, wkn_ref, wkab_ref, tk_ref,
                   selk_ref, k_ref, v_ref):
    proj = _modulated_proj(x_ref[0], mod_ref, nw_ref, win_ref)
    _kv_prep(proj, kvn_ref, wukv_ref, wkn_ref, wkab_ref, tk_ref, selk_ref, k_ref, v_ref)


def _prep_call(x, mod, per_batch, wts, tabs, tm, with_q, op_dtype):
    bsz, seq, _ = x.shape
    assert seq % tm == 0 and tm % CHUNK == 0, (seq, tm)
    x_spec = pl.BlockSpec((1, tm, D_MODEL), lambda b, i: (b, i, 0))
    tab_spec = lambda w: pl.BlockSpec((tm, w), lambda b, i: (i, 0))
    ncols = EXT_COLS if with_q else EXT_KV_ONLY
    kv_specs = [x_spec, _mod_spec(1, per_batch), _const_spec((1, D_MODEL)),
                _const_spec((D_MODEL, ncols)), _const_spec((1, KV_LORA)),
                _const_spec((KV_LORA, MLA_HEADS * LANES)), _const_spec((1, LANES)),
                _const_spec((1, LANES)), tab_spec(LANES), _const_spec((2 * LANES, 2 * LANES))]
    kv_args = [x, mod, wts["norm2"], wts["w_in"] if with_q else wts["w_in"][:, :EXT_KV_ONLY],
               wts["kvn"], wts["w_ukv"], wts["wk_nope"], wts["wk_ab"], tabs["k"], wts["sel_k"]]
    k_shape = jax.ShapeDtypeStruct((bsz, MLA_HEADS, seq, HEAD_PAD), op_dtype)
    v_shape = jax.ShapeDtypeStruct((bsz, seq, MLA_HEADS * LANES), op_dtype)
    sg_shape = jax.ShapeDtypeStruct((bsz, seq, GMLP_WIDTH), jnp.bfloat16)
    k_spec = pl.BlockSpec((1, MLA_HEADS, tm, HEAD_PAD), lambda b, i: (b, 0, i, 0))
    v_spec = pl.BlockSpec((1, tm, MLA_HEADS * LANES), lambda b, i: (b, i, 0))
    sg_spec = pl.BlockSpec((1, tm, GMLP_WIDTH), lambda b, i: (b, i, 0))
    params = pltpu.CompilerParams(dimension_semantics=("arbitrary", "arbitrary"),
                                  vmem_limit_bytes=VMEM_LIMIT)
    if not with_q:
        return pl.pallas_call(
            _kvonly_kernel, grid=(bsz, seq // tm), in_specs=kv_specs,
            out_specs=[k_spec, v_spec], out_shape=[k_shape, v_shape],
            compiler_params=params, name="kv_prep")(*kv_args)
    q_specs = [_const_spec((1, Q_LORA)), _const_spec((Q_LORA, MLA_HEADS * LANES)),
               _const_spec((1, 2 * LANES)), tab_spec(2 * LANES), _const_spec((2 * LANES, 2 * LANES)),
               _const_spec((1, GMLP_WIDTH)), _const_spec((2 * LANES, 2 * LANES)),
               _const_spec((GMLP_GROUPS // 2, CHUNK, 2 * CHUNK)), _const_spec((CHUNK, GMLP_WIDTH))]
    q_args = [wts["qan"], wts["w_uq"], wts["wq"], tabs["q"], wts["sel_q"], wts["vnw"], wts["sel_v"],
              wts["ws"], wts["bs"]]
    return pl.pallas_call(
        _prep_kernel, grid=(bsz, seq // tm), in_specs=kv_specs + q_specs,
        out_specs=[k_spec, v_spec, k_spec, sg_spec],
        out_shape=[k_shape, v_shape, k_shape, sg_shape],
        compiler_params=params, name="mix_prep")(*kv_args, *q_args)


def _attn_kernel(q_ref, kl_ref, kc_ref, vl_ref, vc_ref, *rest):
    n_side = (len(rest) - 1) // 2
    o_ref = rest[n_side]
    for src, dst in zip(rest[:n_side], rest[n_side + 1:]):
        dst[...] = src[...].astype(dst.dtype)
    for r0 in range(0, q_ref.shape[2], Q_SUB):
        _attn_rows(slice(r0, r0 + Q_SUB), q_ref, kl_ref, kc_ref, vl_ref, vc_ref, o_ref)


def _attn_rows(rows, q_ref, kl_ref, kc_ref, vl_ref, vc_ref, o_ref):
    nt = (((1,), (1,)), ((), ()))
    lane = _lane_iota((Q_SUB, LANES))
    lo = lane < 64
    ones_lane = (lane == V_HEAD, lane == 0)
    op_dtype = vl_ref.dtype
    p_shift = P_SHIFT if op_dtype == F8 else 0.0
    outs = []
    for h in range(MLA_HEADS):
        q = q_ref[0, h, rows]
        par = h % 2
        pv = slice((h - par) * LANES, (h - par + 2) * LANES)
        s1 = lax.dot_general(q, kl_ref[0, h], nt, preferred_element_type=jnp.float32)
        s2 = lax.dot_general(q, kc_ref[0, h], nt, preferred_element_type=jnp.float32)
        m = jnp.maximum(jnp.max(s1, axis=-1, keepdims=True),
                        jnp.max(s2, axis=-1, keepdims=True)) - p_shift
        p1 = jnp.exp2(s1 - m).astype(op_dtype)
        p2 = jnp.exp2(s2 - m).astype(op_dtype)
        o = (jnp.dot(p1, vl_ref[0, :, pv], preferred_element_type=jnp.float32)
             + jnp.dot(p2, vc_ref[0, :, pv], preferred_element_type=jnp.float32))
        o = o[:, par * LANES:(par + 1) * LANES]
        l = jnp.sum(jnp.where(ones_lane[par], o, 0.0), axis=-1, keepdims=True)
        outs.append(o / l)
    for p in range(N_PAIRS):
        o_ref[0, rows, p * LANES:(p + 1) * LANES] = jnp.where(
            lo, outs[2 * p], outs[2 * p + 1]).astype(jnp.bfloat16)


def _side_cast_specs(side, n_steps, per_b):
    specs, shapes = [], []
    for w in side:
        rows, cols = w.shape
        blk = next(b for b in range(16, rows + 1, 16) if rows % b == 0 and rows // b <= n_steps)
        last = rows // blk - 1
        specs.append(pl.BlockSpec(
            (blk, cols), lambda b, i, last=last: (jnp.minimum(b * per_b + i, last), 0)))
        shapes.append(jax.ShapeDtypeStruct(w.shape, jnp.bfloat16))
    return specs, shapes


def _attn_call(q, k_lat, k_ctx, v_lat, v_ctx, side, tq):
    bsz, _, seq, _ = q.shape
    assert seq % tq == 0 and tq % Q_SUB == 0, (seq, tq)
    n_ctx = k_ctx.shape[2]
    half = MLA_HEADS * V_HEAD
    side_specs, side_shapes = _side_cast_specs(side, bsz * (seq // tq), seq // tq)
    return pl.pallas_call(
        _attn_kernel,
        grid=(bsz, seq // tq),
        in_specs=[pl.BlockSpec((1, MLA_HEADS, tq, HEAD_PAD), lambda b, i: (b, 0, i, 0)),
                  pl.BlockSpec((1, MLA_HEADS, seq, HEAD_PAD), lambda b, i: (b, 0, 0, 0)),
                  pl.BlockSpec((1, MLA_HEADS, n_ctx, HEAD_PAD), lambda b, i: (b, 0, 0, 0)),
                  pl.BlockSpec((1, seq, MLA_HEADS * LANES), lambda b, i: (b, 0, 0)),
                  pl.BlockSpec((1, n_ctx, MLA_HEADS * LANES), lambda b, i: (b, 0, 0))] + side_specs,
        out_specs=[pl.BlockSpec((1, tq, half), lambda b, i: (b, i, 0))] + side_specs,
        out_shape=[jax.ShapeDtypeStruct((bsz, seq, half), jnp.bfloat16)] + side_shapes,
        compiler_params=pltpu.CompilerParams(
            dimension_semantics=("arbitrary", "arbitrary"),
            vmem_limit_bytes=VMEM_LIMIT),
        name="attention",
    )(q, k_lat, k_ctx, v_lat, v_ctx, *side)


def _out_ffn_kernel(x_ref, attn_ref, sg_ref, modm_ref, wout_ref, mod_ref, nw_ref,
                    w1_ref, w3_ref, w2_ref, o_ref, h_ref, acc_ref):
    y = (jnp.dot(attn_ref[0], wout_ref[0], preferred_element_type=jnp.float32)
         + jnp.dot(sg_ref[0], wout_ref[1], preferred_element_type=jnp.float32))
    x = x_ref[0] + modm_ref[0, 2] * y
    o_ref[0] = _ffn_core(x, mod_ref, nw_ref, w1_ref, w3_ref, w2_ref, h_ref, acc_ref)


def _out_ffn_call(x, attn, sg, mod, w_out, norm_w, w1, w3, w2, tm):
    bsz, seq, _ = x.shape
    assert seq % tm == 0, (seq, tm)
    half = MLA_HEADS * V_HEAD
    row = lambda w: pl.BlockSpec((1, tm, w), lambda b, i: (b, i, 0))
    return pl.pallas_call(
        _out_ffn_kernel,
        grid=(bsz, seq // tm),
        in_specs=[row(D_MODEL), row(half), row(GMLP_WIDTH), _mod_spec(1, True),
                  _const_spec((2, half, D_MODEL)), _mod_spec(2, True), _const_spec((1, D_MODEL)),
                  _const_spec((D_MODEL, D_FF)), _const_spec((D_MODEL, D_FF)),
                  _const_spec((N_FF, FF_TILE, D_MODEL))],
        out_specs=row(D_MODEL),
        out_shape=jax.ShapeDtypeStruct(x.shape, jnp.float32),
        scratch_shapes=[pltpu.VMEM((tm, D_MODEL), jnp.bfloat16),
                        pltpu.VMEM((tm, D_MODEL), jnp.float32)],
        compiler_params=pltpu.CompilerParams(dimension_semantics=("arbitrary", "arbitrary"),
                                             vmem_limit_bytes=VMEM_LIMIT),
        name="out_ffn",
    )(x, attn, sg, mod, w_out, mod, norm_w, w1, w3, w2)


def _ffn_weights(w1, w3, w2):
    bf = jnp.bfloat16
    return w1.astype(bf), w3.astype(bf), w2.astype(bf).reshape(N_FF, FF_TILE, D_MODEL)


def _rot_cols(w, start, signed=True):
    half = AXIS_DIM // 2
    parts = []
    for blk in range(QK_ROPE // half):
        src = start + (blk + 1) * half if blk % 2 == 0 else start + (blk - 1) * half
        piece = w[..., src:src + half]
        parts.append(-piece if (signed and blk % 2 == 0) else piece)
    return jnp.concatenate(parts, axis=-1)


def _selectors():
    r = np.arange(2 * LANES)[:, None]
    c = np.arange(2 * LANES)[None, :]
    sel_k = ((r < 64) & (c < LANES)) | ((r >= 64) & (r < LANES) & (c >= LANES)) | (
        (r >= LANES) & (r < LANES + QK_ROPE))
    ro = r - LANES
    sel_q = ((r < QK_HEAD) & (c < LANES)) | (
        (r >= LANES) & ((ro < QK_ROPE) | (ro >= 2 * QK_ROPE)) & (c >= LANES))
    sel_v = (r // GMLP_GROUP_DIM) == (c // GMLP_GROUP_DIM)
    return [jnp.asarray(m, jnp.bfloat16) for m in (sel_k, sel_q, sel_v)]


def _mix_weights(norm2_w, w_in, q_a_norm_w, w_uq, kv_a_norm_w, w_ukv, q_norm_w, k_norm_w,
                 v_norm_w, w_s, b_s):
    bf = jnp.bfloat16
    kpe = w_in[:, KV_LORA:KV_COLS]
    kpe_rot = _rot_cols(w_in, KV_LORA)
    w_in_ext = jnp.concatenate([w_in[:, :KV_LORA], kpe, kpe, kpe_rot, kpe_rot,
                                w_in[:, Q_START:]], axis=1).astype(bf)
    ukv = w_ukv.reshape(KV_LORA, MLA_HEADS, QK_NOPE + V_HEAD)
    w_ukv_p = jnp.concatenate([ukv[:, :, :QK_NOPE].reshape(KV_LORA, -1),
                               ukv[:, :, QK_NOPE:].reshape(KV_LORA, -1)], axis=1).astype(bf)
    cols = []
    for h in range(MLA_HEADS):
        base = h * QK_HEAD
        nope = w_uq[:, base:base + QK_NOPE]
        pes = w_uq[:, base + QK_NOPE:base + QK_HEAD]
        rots = _rot_cols(w_uq, base + QK_NOPE)
        cols += [nope, pes, rots] if h % 2 == 0 else [pes, rots, nope]
    w_uq_ext = jnp.concatenate(cols, axis=1).astype(bf)
    qn, qp, qr = q_norm_w[:QK_NOPE], q_norm_w[QK_NOPE:], _rot_cols(q_norm_w, QK_NOPE, signed=False)
    wq = jnp.concatenate([qn, qp, qr, qp, qr, qn])[None]
    kn, kp, kr = k_norm_w[:QK_NOPE], k_norm_w[QK_NOPE:], _rot_cols(k_norm_w, QK_NOPE, signed=False)
    sel_k, sel_q, sel_v = _selectors()
    return dict(
        sel_k=sel_k, sel_q=sel_q, sel_v=sel_v,
        norm2=norm2_w[None], w_in=w_in_ext, kvn=kv_a_norm_w[None], w_ukv=w_ukv_p,
        wk_nope=jnp.concatenate([kn, kn])[None],
        wk_ab=jnp.concatenate([kp, kp, kr, kr])[None],
        qan=q_a_norm_w[None], w_uq=w_uq_ext, wq=wq,
        vnw=v_norm_w.reshape(1, GMLP_WIDTH),
        ws=w_s.astype(bf).reshape(GMLP_GROUPS // 2, 2, CHUNK, CHUNK).transpose(0, 2, 1, 3
                                   ).reshape(GMLP_GROUPS // 2, CHUNK, 2 * CHUNK),
        bs=jnp.broadcast_to(b_s.T[:, :, None], (CHUNK, GMLP_GROUPS, GMLP_GROUP_DIM)
                            ).reshape(CHUNK, GMLP_WIDTH),
    )


def _rope_tables(seq, n_ctx):
    f32 = jnp.float32
    rows_n = seq // GRID_W
    rows = jnp.repeat(jnp.arange(rows_n, dtype=f32), GRID_W)
    cols = jnp.tile(jnp.arange(GRID_W, dtype=f32), rows_n)
    inv = ROPE_BASE ** (-jnp.arange(0, AXIS_DIM, 2, dtype=f32) / AXIS_DIM)
    ang_r = rows[:, None] * inv
    ang_c = cols[:, None] * inv
    ang = jnp.concatenate([ang_r, ang_r, ang_c, ang_c], axis=-1)
    cos, sin = jnp.cos(ang), jnp.sin(ang)
    one = jnp.ones((seq, 64), f32)
    lat = dict(k=jnp.concatenate([cos, cos, sin, sin], 1),
               q=jnp.concatenate([one, cos, sin, cos, sin, one], 1))
    ctx = dict(k=jnp.concatenate([jnp.ones((n_ctx, 64), f32), jnp.zeros((n_ctx, 64), f32)], 1))
    return lat, ctx


def kernel(x, c, ctx, c_ctx, w_ada, b_ada, norm1_w, ffn1_w1, ffn1_w3, ffn1_w2, norm2_w, w_in,
           q_a_norm_w, w_uq, kv_a_norm_w, w_ukv, q_norm_w, k_norm_w, v_norm_w, w_s, b_s, w_out,
           norm3_w, ffn2_w1, ffn2_w3, ffn2_w2):
    bsz, seq, _ = x.shape
    n_ctx = ctx.shape[1]
    rows = -(-(bsz + 1) // 8) * 8
    cc = jnp.concatenate([c, c_ctx[None], jnp.zeros((rows - bsz - 1, D_MODEL), jnp.float32)], 0)
    mod = _ada_call(cc, w_ada[0], b_ada[0][None]).reshape(rows, N_MOD, 1, D_MODEL)
    mod_ctx = mod[bsz:bsz + 1]

    f1 = _ffn_weights(ffn1_w1[0], ffn1_w3[0], ffn1_w2[0])
    side = (ffn2_w1[0], ffn2_w3[0], ffn2_w2[0], w_out[0])
    wts = _mix_weights(norm2_w[0], w_in[0], q_a_norm_w[0], w_uq[0], kv_a_norm_w[0], w_ukv[0],
                       q_norm_w[0], k_norm_w[0], v_norm_w[0], w_s[0], b_s[0])
    tabs_lat, tabs_ctx = _rope_tables(seq, n_ctx)

    x1 = _ffn_call(x, mod, 0, True, norm1_w, *f1, tm=ROW_TILE)
    ctx1 = _ffn_call(ctx.reshape(1, bsz * n_ctx, D_MODEL), mod_ctx, 0, False, norm1_w, *f1,
                     tm=ROW_TILE).reshape(bsz, n_ctx, D_MODEL)

    def mix(op_dtype):
        k_lat, v_lat, q, sg = _prep_call(x1, mod, True, wts, tabs_lat, PREP_TILE, True, op_dtype)
        k_ctx, v_ctx = _prep_call(ctx1, mod_ctx, False, wts, tabs_ctx, n_ctx, False, op_dtype)
        return (*_attn_call(q, k_lat, k_ctx, v_lat, v_ctx, side, tq=Q_TILE), sg)

    logit_bound = (QK_HEAD ** 0.5) * jnp.max(jnp.abs(q_norm_w[0])) * jnp.max(jnp.abs(k_norm_w[0]))
    attn, w1b, w3b, w2b, wob, sg = lax.cond(logit_bound <= F8_LOGIT_BOUND,
                                            lambda: mix(F8), lambda: mix(jnp.bfloat16))
    w_out_r = wob.reshape(2, MLA_HEADS * V_HEAD, D_MODEL)
    return _out_ffn_call(x1, attn, sg, mod, w_out_r, norm3_w, w1b, w3b,
                         w2b.reshape(N_FF, FF_TILE, D_MODEL), tm=ROW_TILE)
```
